```python
import math
import jax
import jax.numpy as jnp
from jax import lax
import numpy as np

D_MODEL = 1024
BATCH = 16
SEQ = 4096
DEPTH = 4

GRID_W = 64
NA_HEAD_DIM = 64
NA_WIDTH = D_MODEL // 2
NA_HEADS = NA_WIDTH // NA_HEAD_DIM
NA_WIN_R = 8
NA_WIN_C = 16
HY_WIDTH = D_MODEL // 2
HY_SHORT = 3
HY_EMB = 33
HY_FFN = 64
HY_TARGET = 1e-2
HY_FAST_PCT = 0.3
HY_SLOW_PCT = 1.5
SG_WIDTH = D_MODEL // 2
SG_GROUPS = 8
SG_CHUNK = 128
CV_WIDTH = D_MODEL // 2
CV_KERNEL = 31
MLP_HIDDEN = 4 * D_MODEL
N_EVEN = (DEPTH + 1) // 2
N_ODD = DEPTH // 2
EPS = 1e-6

kernel_name = 'hybrid_natten_hyena_gmlp_conformer_encoder'


def rms_norm(x, g):
    xf = x.astype(jnp.float32)
    y = xf * lax.rsqrt(jnp.mean(xf * xf, axis=-1, keepdims=True) + EPS)
    return (y * g.astype(jnp.float32)).astype(x.dtype)


def layer_norm(x, g, b):
    xf = x.astype(jnp.float32)
    xc = xf - jnp.mean(xf, axis=-1, keepdims=True)
    y = xc * lax.rsqrt(jnp.mean(xc * xc, axis=-1, keepdims=True) + EPS)
    return (y * g.astype(jnp.float32) + b.astype(jnp.float32)).astype(x.dtype)


def depthwise_conv(x, w, b):
    k = w.shape[0]
    y = lax.conv_general_dilated(x, w[:, None, :].astype(x.dtype), window_strides=(1,),
                                 padding=[(k // 2, k // 2)],
                                 dimension_numbers=('NWC', 'WIO', 'NWC'),
                                 feature_group_count=x.shape[-1])
    return y + b.astype(x.dtype)


def neighbourhood_attention(qkv, rpb):
    bsz, seq_len, _ = qkv.shape
    rows = seq_len // GRID_W
    kr = min(NA_WIN_R, rows)
    q, k, v = jnp.split(qkv, 3, axis=-1)
    grid = (bsz, rows, GRID_W, NA_HEADS, NA_HEAD_DIM)
    q = q.reshape(grid) * (NA_HEAD_DIM ** -0.5)
    k = k.reshape(grid)
    v = v.reshape(grid)
    cs = np.clip(np.arange(GRID_W) - NA_WIN_C // 2, 0, GRID_W - NA_WIN_C)
    col_idx = cs[:, None] + np.arange(NA_WIN_C)[None, :]
    col_off = col_idx - np.arange(GRID_W)[:, None] + (NA_WIN_C - 1)
    rpb_c = rpb[:, :, col_off]

    def row_block(r):
        rs = jnp.clip(r - kr // 2, 0, rows - kr)
        q_r = lax.dynamic_index_in_dim(q, r, axis=1, keepdims=False)
        k_rows = lax.dynamic_slice_in_dim(k, rs, kr, axis=1)
        v_rows = lax.dynamic_slice_in_dim(v, rs, kr, axis=1)
        k_win = k_rows[:, :, col_idx]
        v_win = v_rows[:, :, col_idx]
        row_off = rs + jnp.arange(kr) - r + (NA_WIN_R - 1)
        bias = jnp.take(rpb_c, row_off, axis=1).transpose(0, 2, 1, 3)
        s = jnp.einsum('bqhd,brqchd->bhqrc', q_r, k_win).astype(jnp.float32)
        s = s + bias.astype(jnp.float32)[None]
        p = jax.nn.softmax(s.reshape(bsz, NA_HEADS, GRID_W, kr * NA_WIN_C), axis=-1)
        p = p.reshape(s.shape).astype(v.dtype)
        return jnp.einsum('bhqrc,brqchd->bqhd', p, v_win)

    out = lax.map(row_block, jnp.arange(rows))
    return out.transpose(1, 0, 2, 3, 4).reshape(bsz, seq_len, NA_WIDTH)


def hyena_filters(seq_len, w1, b1, w2, b2, w3, b3, freq, f_out):
    f32 = jnp.float32
    pos = jnp.arange(seq_len, dtype=f32)
    t = (pos / (seq_len - 1))[:, None]
    bands = (HY_EMB - 1) // 2
    fr = jnp.linspace(1e-4, bands - 1, bands, dtype=f32)
    ang = (2.0 * math.pi * pos / seq_len)[:, None] * fr[None, :]
    z = jnp.concatenate([t, jnp.cos(ang), -jnp.sin(ang)], axis=-1)
    fq = freq.astype(f32)
    hid = jnp.sin(fq * (z @ w1.astype(f32) + b1.astype(f32)))
    hid = jnp.sin(fq * (hid @ w2.astype(f32) + b2.astype(f32)))
    hid = jnp.sin(fq * (hid @ w3.astype(f32) + b3.astype(f32)))
    h = hid @ f_out.astype(f32)
    deltas = jnp.abs(jnp.linspace(math.log(HY_TARGET) / HY_SLOW_PCT,
                                  math.log(HY_TARGET) / HY_FAST_PCT, HY_WIDTH, dtype=f32))
    h = h * jnp.exp(-t * jnp.tile(deltas, 2)[None, :])
    h_f, h_b = jnp.split(h, 2, axis=-1)
    scale = lax.rsqrt(jnp.sum(h_f * h_f, axis=0) + jnp.sum(h_b * h_b, axis=0))
    return h_f * scale, h_b * scale


def hyena_mixer(z, conv_w, conv_b, w1, b1, w2, b2, w3, b3, freq, f_out, skip):
    seq_len = z.shape[1]
    z = depthwise_conv(z, conv_w, conv_b)
    x0, x1, v = jnp.split(z, 3, axis=-1)
    h_f, h_b = hyena_filters(seq_len, w1, b1, w2, b2, w3, b3, freq, f_out)
    u = (v * x1).astype(jnp.float32)
    n = 2 * seq_len
    u_hat = jnp.fft.rfft(u, n=n, axis=1)
    h_hat = jnp.fft.rfft(h_f, n=n, axis=0) + jnp.conj(jnp.fft.rfft(h_b, n=n, axis=0))
    y = jnp.fft.irfft(u_hat * h_hat[None], n=n, axis=1)[:, :seq_len]
    y = y + u * skip.astype(jnp.float32)
    return (y * x0.astype(jnp.float32)).astype(z.dtype)


def even_mixer(h, w_in, rpb, conv_w, conv_b, w1, b1, w2, b2, w3, b3, freq, f_out, skip, w_out):
    p = h @ w_in
    att = neighbourhood_attention(p[..., :3 * NA_WIDTH], rpb)
    hy = hyena_mixer(p[..., 3 * NA_WIDTH:], conv_w, conv_b, w1, b1, w2, b2, w3, b3,
                     freq, f_out, skip)
    return jnp.concatenate([att, hy], axis=-1) @ w_out


def odd_mixer(h, w_in, ln_g, ln_b, sg_w, sg_b, dw_w, dw_b, cln_g, cln_b, w_out):
    bsz, seq_len, _ = h.shape
    p = h @ w_in
    zu, zv, za, zg = jnp.split(p, [SG_WIDTH, 2 * SG_WIDTH, 2 * SG_WIDTH + CV_WIDTH], axis=-1)
    u = jax.nn.gelu(zu, approximate=False)
    v = layer_norm(jax.nn.gelu(zv, approximate=False), ln_g, ln_b)
    v = v.reshape(bsz, seq_len // SG_CHUNK, SG_CHUNK, SG_GROUPS, SG_WIDTH // SG_GROUPS)
    v = jnp.einsum('gpq,bnqgc->bnpgc', sg_w, v) + sg_b.T[:, :, None]
    c_out = u * v.reshape(bsz, seq_len, SG_WIDTH)
    a = za * jax.nn.sigmoid(zg)
    a = jax.nn.silu(layer_norm(depthwise_conv(a, dw_w, dw_b), cln_g, cln_b))
    return jnp.concatenate([c_out, a], axis=-1) @ w_out


def squared_relu_mlp(h, w1, w2):
    return jnp.square(jax.nn.relu(h @ w1)) @ w2


def setup_inputs(seed: int = 0) -> dict:
    key = jax.random.key(seed)
    ks = iter(jax.random.split(key, 32))

    def nrm(shape, scale):
        return scale * jax.random.normal(next(ks), shape, jnp.float32)

    d = D_MODEL
    return {
        'x': nrm((BATCH, SEQ, d), 1.0),
        'norm_g': 1.0 + nrm((DEPTH, 2, d), 0.01),
        'ev_w_in': nrm((N_EVEN, d, 3 * NA_WIDTH + 3 * HY_WIDTH), d ** -0.5),
        'ev_rpb': nrm((N_EVEN, NA_HEADS, 2 * NA_WIN_R - 1, 2 * NA_WIN_C - 1), 0.1),
        'hy_conv_w': nrm((N_EVEN, HY_SHORT, 3 * HY_WIDTH), HY_SHORT ** -0.5),
        'hy_conv_b': nrm((N_EVEN, 3 * HY_WIDTH), 0.01),
        'hy_w1': nrm((N_EVEN, HY_EMB, HY_FFN), HY_EMB ** -0.5),
        'hy_b1': nrm((N_EVEN, HY_FFN), 0.02),
        'hy_w2': nrm((N_EVEN, HY_FFN, HY_FFN), HY_FFN ** -0.5),
        'hy_b2': nrm((N_EVEN, HY_FFN), 0.02),
        'hy_w3': nrm((N_EVEN, HY_FFN, HY_FFN), HY_FFN ** -0.5),
        'hy_b3': nrm((N_EVEN, HY_FFN), 0.02),
        'hy_freq': 1.0 + nrm((N_EVEN, HY_FFN), 0.01),
        'hy_w_out': nrm((N_EVEN, HY_FFN, 2 * HY_WIDTH), HY_FFN ** -0.5),
        'hy_skip': nrm((N_EVEN, HY_WIDTH), 0.5),
        'ev_w_out': nrm((N_EVEN, NA_WIDTH + HY_WIDTH, d), (NA_WIDTH + HY_WIDTH) ** -0.5),
        'od_w_in': nrm((N_ODD, d, 2 * SG_WIDTH + 2 * CV_WIDTH), d ** -0.5),
        'sg_ln_g': 1.0 + nrm((N_ODD, SG_WIDTH), 0.01),
        'sg_ln_b': nrm((N_ODD, SG_WIDTH), 0.01),
        'sg_w': nrm((N_ODD, SG_GROUPS, SG_CHUNK, SG_CHUNK), SG_CHUNK ** -0.5),
        'sg_b': 1.0 + nrm((N_ODD, SG_GROUPS, SG_CHUNK), 0.01),
        'cv_dw_w': nrm((N_ODD, CV_KERNEL, CV_WIDTH), CV_KERNEL ** -0.5),
        'cv_dw_b': nrm((N_ODD, CV_WIDTH), 0.01),
        'cv_ln_g': 1.0 + nrm((N_ODD, CV_WIDTH), 0.01),
        'cv_ln_b': nrm((N_ODD, CV_WIDTH), 0.01),
        'od_w_out': nrm((N_ODD, SG_WIDTH + CV_WIDTH, d), (SG_WIDTH + CV_WIDTH) ** -0.5),
        'mlp_w1': nrm((DEPTH, d, MLP_HIDDEN), d ** -0.5),
        'mlp_w2': nrm((DEPTH, MLP_HIDDEN, d), MLP_HIDDEN ** -0.5),
        'final_g': 1.0 + nrm((d,), 0.01),
    }


def reference(x, norm_g, ev_w_in, ev_rpb, hy_conv_w, hy_conv_b, hy_w1, hy_b1, hy_w2, hy_b2,
              hy_w3, hy_b3, hy_freq, hy_w_out, hy_skip, ev_w_out, od_w_in, sg_ln_g, sg_ln_b,
              sg_w, sg_b, cv_dw_w, cv_dw_b, cv_ln_g, cv_ln_b, od_w_out, mlp_w1, mlp_w2, final_g):
    for i in range(DEPTH):
        j = i // 2
        hn = rms_norm(x, norm_g[i, 0])
        if i % 2 == 0:
            mix = even_mixer(hn, ev_w_in[j], ev_rpb[j], hy_conv_w[j], hy_conv_b[j],
                             hy_w1[j], hy_b1[j], hy_w2[j], hy_b2[j], hy_w3[j], hy_b3[j],
                             hy_freq[j], hy_w_out[j], hy_skip[j], ev_w_out[j])
        else:
            mix = odd_mixer(hn, od_w_in[j], sg_ln_g[j], sg_ln_b[j], sg_w[j], sg_b[j],
                            cv_dw_w[j], cv_dw_b[j], cv_ln_g[j], cv_ln_b[j], od_w_out[j])
        x = x + mix
        x = x + squared_relu_mlp(rms_norm(x, norm_g[i, 1]), mlp_w1[i], mlp_w2[i])
    return rms_norm(x, final_g)
```

```python
import functools
import math

import numpy as np
import jax
import jax.numpy as jnp
from jax import lax
from jax.experimental import pallas as pl
from jax.experimental.pallas import tpu as pltpu

F32 = jnp.float32
BF16 = jnp.bfloat16
EPS = 1e-6
MASK_VALUE = -1e30

GRID_W = 64
NA_HEAD_DIM = 64
NA_WIN_R = 8
NA_WIN_C = 16
HY_EMB = 33
HY_TARGET = 1e-2
HY_FAST_PCT = 0.3
HY_SLOW_PCT = 1.5
SG_CHUNK = 128
CV_KERNEL = 31

LANES = 128
SUBLANES = 8
VMEM_LIMIT = 56 * 1024 * 1024


def _params(*sem):
    return pltpu.CompilerParams(dimension_semantics=sem, vmem_limit_bytes=VMEM_LIMIT)


def _resident(shape, index_map):
    return pl.BlockSpec(shape, index_map, pipeline_mode=pl.Buffered(1))


def _rms(x, g):
    return x * lax.rsqrt(jnp.mean(x * x, axis=-1, keepdims=True) + EPS) * g


def _layer_norm(x, g, b):
    xc = x - jnp.mean(x, axis=-1, keepdims=True)
    return xc * lax.rsqrt(jnp.mean(xc * xc, axis=-1, keepdims=True) + EPS) * g + b


def _gelu(x):
    return x * (lax.erf(x / np.float32(math.sqrt(2.0))) + 1.0) / 2.0


def _norm_proj_kernel(x_ref, g_ref, w_ref, o_ref):
    hn = _rms(x_ref[...], g_ref[...]).astype(BF16)
    o_ref[...] = jnp.dot(hn, w_ref[...], preferred_element_type=F32).astype(o_ref.dtype)


def _norm_proj(x2, g, w, tm=512):
    m, d = x2.shape
    n = w.shape[1]
    return pl.pallas_call(
        _norm_proj_kernel,
        grid=(m // tm,),
        in_specs=[pl.BlockSpec((tm, d), lambda i: (i, 0)),
                  _resident((1, d), lambda i: (0, 0)),
                  _resident((d, n), lambda i: (0, 0))],
        out_specs=pl.BlockSpec((tm, n), lambda i: (i, 0)),
        out_shape=jax.ShapeDtypeStruct((m, n), BF16),
        compiler_params=_params("parallel"),
        name="norm_proj",
    )(x2, g.reshape(1, d), w)


def _na_kernel(q_ref, k_ref, v_ref, tb_ref, o_ref):
    seq_len = q_ref.shape[1]
    rows = seq_len // GRID_W
    nkeys = NA_WIN_R * GRID_W
    lane = lax.broadcasted_iota(jnp.int32, (GRID_W, LANES), 1)
    first = lane < NA_HEAD_DIM

    def body(r, carry):
        rs = jnp.clip(r - NA_WIN_R // 2, 0, rows - NA_WIN_R)
        case = r - rs
        q0 = pl.multiple_of(r * GRID_W, GRID_W)
        k0 = pl.multiple_of(rs * GRID_W, GRID_W)
        q2 = q_ref[0, pl.ds(q0, GRID_W), :] * BF16(NA_HEAD_DIM ** -0.5)
        k2 = k_ref[0, pl.ds(k0, nkeys), :]
        v2 = v_ref[0, pl.ds(k0, nkeys), :]
        zero = jnp.zeros_like(q2)
        outs = []
        for hh in range(2):
            qh = jnp.where(first if hh == 0 else jnp.logical_not(first), q2, zero)
            s = lax.dot_general(qh, k2, (((1,), (1,)), ((), ())), preferred_element_type=F32)
            s = s + tb_ref[0, hh, case]
            e = jnp.exp(s - jnp.max(s, axis=-1, keepdims=True))
            den = jnp.sum(e, axis=-1, keepdims=True)
            o = jnp.dot(e.astype(BF16), v2, preferred_element_type=F32)
            outs.append(o / den)
        o_ref[0, pl.ds(q0, GRID_W), :] = jnp.where(first, outs[0], outs[1]).astype(o_ref.dtype)
        return carry

    lax.fori_loop(0, rows, body, 0)


def _na_bias_table(rpb):
    heads = rpb.shape[0]
    case = np.arange(NA_WIN_R)[:, None]
    kr = np.arange(NA_WIN_R)[None, :]
    row_idx = kr - case + (NA_WIN_R - 1)
    qc = np.arange(GRID_W)[:, None]
    kc = np.arange(GRID_W)[None, :]
    cs = np.clip(qc - NA_WIN_C // 2, 0, GRID_W - NA_WIN_C)
    inside = (kc >= cs) & (kc < cs + NA_WIN_C)
    col_idx = np.clip(kc - qc + (NA_WIN_C - 1), 0, 2 * NA_WIN_C - 2)
    t = rpb.astype(F32)[:, row_idx[:, :, None, None], col_idx[None, None, :, :]]
    t = jnp.where(inside[None, None, None], t, MASK_VALUE)
    t = t.transpose(0, 1, 3, 2, 4).reshape(heads // 2, 2, NA_WIN_R, GRID_W, NA_WIN_R * GRID_W)
    return t


def _neighbourhood_attention(p3, rpb):
    bsz, seq_len, _ = p3.shape
    heads = rpb.shape[0]
    pairs = heads // 2
    tb = _na_bias_table(rpb)
    blk = (1, seq_len, LANES)
    return pl.pallas_call(
        _na_kernel,
        grid=(bsz, pairs),
        in_specs=[pl.BlockSpec(blk, lambda b, j: (b, 0, j)),
                  pl.BlockSpec(blk, lambda b, j: (b, 0, pairs + j)),
                  pl.BlockSpec(blk, lambda b, j: (b, 0, 2 * pairs + j)),
                  pl.BlockSpec((1, 2, NA_WIN_R, GRID_W, NA_WIN_R * GRID_W), lambda b, j: (j, 0, 0, 0, 0))],
        out_specs=pl.BlockSpec(blk, lambda b, j: (b, 0, j)),
        out_shape=jax.ShapeDtypeStruct((bsz, seq_len, pairs * LANES), BF16),
        compiler_params=_params("parallel", "parallel"),
        name="na_attention",
    )(p3, p3, p3, tb)


def _hy_filter_kernel(frq_ref, w1_ref, b1_ref, w2_ref, b2_ref, w3_ref, b3_ref, fq_ref, fo_ref,
                      dl_ref, h_ref, ss_ref, *, seq_len):
    i = pl.program_id(0)
    tl = h_ref.shape[0]
    hp = lax.Precision.HIGHEST
    pos = (lax.broadcasted_iota(jnp.int32, (tl, LANES), 0) + i * tl).astype(F32)
    lane = lax.broadcasted_iota(jnp.int32, (tl, LANES), 1)
    bands = (HY_EMB - 1) // 2
    t = pos / np.float32(seq_len - 1)
    ang = (np.float32(2.0 * math.pi) * pos / np.float32(seq_len)) * frq_ref[...]
    z = jnp.where(lane == 0, t,
                  jnp.where(lane <= bands, jnp.cos(ang),
                            jnp.where(lane <= 2 * bands, -jnp.sin(ang), 0.0)))
    fq = fq_ref[...]
    hid = jnp.sin(fq * (jnp.dot(z, w1_ref[...], precision=hp, preferred_element_type=F32) + b1_ref[...]))
    hid = jnp.sin(fq * (jnp.dot(hid, w2_ref[...], precision=hp, preferred_element_type=F32) + b2_ref[...]))
    hid = jnp.sin(fq * (jnp.dot(hid, w3_ref[...], precision=hp, preferred_element_type=F32) + b3_ref[...]))
    h = jnp.dot(hid, fo_ref[...], precision=hp, preferred_element_type=F32)
    h = h * jnp.exp(-t[:, :1] * dl_ref[...])
    h_ref[...] = h.astype(h_ref.dtype)

    @pl.when(i == 0)
    def _():
        ss_ref[...] = jnp.zeros_like(ss_ref)

    ss_ref[...] += jnp.sum(h * h, axis=0, keepdims=True)


def _pad2(a, rows, cols):
    return jnp.pad(a.astype(F32), ((0, rows - a.shape[0]), (0, cols - a.shape[1])))


def _hyena_filters(seq_len, w1, b1, w2, b2, w3, b3, freq, f_out, tl=512):
    ffn = w1.shape[1]
    width2 = f_out.shape[1]
    bands = (HY_EMB - 1) // 2
    fr = np.linspace(1e-4, bands - 1, bands, dtype=np.float32)
    frq = np.zeros((1, LANES), np.float32)
    frq[0, 1:1 + bands] = fr
    frq[0, 1 + bands:1 + 2 * bands] = fr
    deltas = np.abs(np.linspace(math.log(HY_TARGET) / HY_SLOW_PCT, math.log(HY_TARGET) / HY_FAST_PCT,
                                width2 // 2, dtype=np.float32))
    dl = np.tile(deltas, 2)[None, :]
    row = lambda a: _pad2(a.reshape(1, -1), 1, LANES)
    small = lambda shape: _resident(shape, lambda i: (0, 0))
    return pl.pallas_call(
        functools.partial(_hy_filter_kernel, seq_len=seq_len),
        grid=(seq_len // tl,),
        in_specs=[small((1, LANES)), small((LANES, LANES)), small((1, LANES)), small((LANES, LANES)),
                  small((1, LANES)), small((LANES, LANES)), small((1, LANES)), small((1, LANES)),
                  small((LANES, width2)), small((1, width2))],
        out_specs=[pl.BlockSpec((tl, width2), lambda i: (i, 0)),
                   pl.BlockSpec((1, width2), lambda i: (0, 0))],
        out_shape=[jax.ShapeDtypeStruct((seq_len, width2), BF16),
                   jax.ShapeDtypeStruct((1, width2), F32)],
        compiler_params=_params("arbitrary"),
        name="hyena_filters",
    )(jnp.asarray(frq), _pad2(w1, LANES, LANES), row(b1), _pad2(w2, LANES, LANES), row(b2),
      _pad2(w3, LANES, LANES), row(b3), row(freq), _pad2(f_out, LANES, width2), jnp.asarray(dl))


def _dft_matrix(seq_len, tile_bins):
    n = 2 * seq_len
    k = jnp.arange(seq_len, dtype=jnp.int32)
    t = jnp.arange(seq_len, dtype=jnp.int32)
    phase = ((2 * k[:, None] + 1) * t[None, :]) % (2 * n)
    ang = phase.astype(F32) * np.float32(2.0 * math.pi / (2 * n))
    c = jnp.cos(ang).reshape(seq_len // tile_bins, 1, tile_bins, seq_len)
    s = (-jnp.sin(ang)).reshape(seq_len // tile_bins, 1, tile_bins, seq_len)
    return jnp.concatenate([c, s], axis=1).reshape(n, seq_len).astype(BF16)


def _hy_pre_kernel(x0_ref, x1_ref, v_ref, w_ref, b_ref, u_ref, x0c_ref):
    seq_len = x0_ref.shape[1]
    row = lax.broadcasted_iota(jnp.int32, (seq_len, LANES), 0)

    def conv(z_ref, i):
        z = z_ref[0].astype(F32)
        zm = jnp.where(row == 0, 0.0, pltpu.roll(z, 1, 0))
        zp = jnp.where(row == seq_len - 1, 0.0, pltpu.roll(z, seq_len - 1, 0))
        w = w_ref[i]
        return w[0:1] * zm + w[1:2] * z + w[2:3] * zp + b_ref[i]

    x0c_ref[0] = conv(x0_ref, 0).astype(x0c_ref.dtype)
    u_ref[0] = (conv(v_ref, 2) * conv(x1_ref, 1)).astype(u_ref.dtype)


def _hyena_pre(p3, lane0, conv_w, conv_b):
    bsz, seq_len, _ = p3.shape
    width = conv_w.shape[1] // 3
    nt = width // LANES
    off = lane0 // LANES
    w3 = conv_w.astype(F32).reshape(conv_w.shape[0], 3, width).transpose(1, 0, 2)
    b3 = conv_b.astype(F32).reshape(3, 1, width)
    blk = (1, seq_len, LANES)
    out = jax.ShapeDtypeStruct((bsz, seq_len, width), BF16)
    return pl.pallas_call(
        _hy_pre_kernel,
        grid=(bsz, nt),
        in_specs=[pl.BlockSpec(blk, lambda b, j: (b, 0, off + j)),
                  pl.BlockSpec(blk, lambda b, j: (b, 0, off + nt + j)),
                  pl.BlockSpec(blk, lambda b, j: (b, 0, off + 2 * nt + j)),
                  pl.BlockSpec((3, conv_w.shape[0], LANES), lambda b, j: (0, 0, j)),
                  pl.BlockSpec((3, 1, LANES), lambda b, j: (0, 0, j))],
        out_specs=[pl.BlockSpec(blk, lambda b, j: (b, 0, j)),
                   pl.BlockSpec(blk, lambda b, j: (b, 0, j))],
        out_shape=[out, out],
        compiler_params=_params("parallel", "parallel"),
        name="hyena_pre",
    )(p3, p3, p3, w3, b3)


def _hy_filter_dft_kernel(f_ref, h_ref, ss_ref, o_ref):
    acc = jnp.dot(f_ref[...], h_ref[...], preferred_element_type=F32)
    tb = acc.shape[0] // 2
    hy = acc.shape[1] // 2
    ss = ss_ref[...]
    scale = lax.rsqrt(ss[:, :hy] + ss[:, hy:])
    o_ref[:tb, :] = (acc[:tb, :hy] + acc[:tb, hy:]) * scale
    o_ref[tb:, :] = (acc[tb:, :hy] - acc[tb:, hy:]) * scale


def _hy_fwd_kernel(f_ref, u_ref, h_ref, y_ref):
    acc = jnp.dot(f_ref[...], u_ref[0], preferred_element_type=F32)
    tb = acc.shape[0] // 2
    re, im = acc[:tb], acc[tb:]
    hre, him = h_ref[:tb, :], h_ref[tb:, :]
    y_ref[0, :tb, :] = (re * hre - im * him).astype(y_ref.dtype)
    y_ref[0, tb:, :] = (re * him + im * hre).astype(y_ref.dtype)


def _hy_inv_kernel(g_ref, y_ref, u_ref, x0_ref, skip_ref, o_ref, *, scale):
    y = jnp.dot(g_ref[...], y_ref[0], preferred_element_type=F32) * np.float32(scale)
    y = y + u_ref[0].astype(F32) * skip_ref[...]
    o_ref[0] = (y * x0_ref[0].astype(F32)).astype(o_ref.dtype)


def _hyena_long_conv(u, x0c, h, ss, skip, tile_bins=256, tl=512):
    bsz, seq_len, width = u.shape
    n = 2 * seq_len
    tm = 2 * tile_bins
    fm = _dft_matrix(seq_len, tile_bins)
    gm = fm.T
    hhat = pl.pallas_call(
        _hy_filter_dft_kernel,
        grid=(n // tm,),
        in_specs=[pl.BlockSpec((tm, seq_len), lambda m: (m, 0)),
                  _resident((seq_len, 2 * width), lambda m: (0, 0)),
                  _resident((1, 2 * width), lambda m: (0, 0))],
        out_specs=pl.BlockSpec((tm, width), lambda m: (m, 0)),
        out_shape=jax.ShapeDtypeStruct((n, width), F32),
        compiler_params=_params("parallel"),
        name="hyena_filter_dft",
    )(fm, h, ss)
    yhat = pl.pallas_call(
        _hy_fwd_kernel,
        grid=(n // tm, bsz),
        in_specs=[pl.BlockSpec((tm, seq_len), lambda m, b: (m, 0)),
                  pl.BlockSpec((1, seq_len, width), lambda m, b: (b, 0, 0)),
                  pl.BlockSpec((tm, width), lambda m, b: (m, 0))],
        out_specs=pl.BlockSpec((1, tm, width), lambda m, b: (b, m, 0)),
        out_shape=jax.ShapeDtypeStruct((bsz, n, width), BF16),
        compiler_params=_params("parallel", "parallel"),
        name="hyena_fwd_dft",
    )(fm, u, hhat)
    tile = pl.BlockSpec((1, tl, width), lambda b, m: (b, m, 0))
    return pl.pallas_call(
        functools.partial(_hy_inv_kernel, scale=2.0 / n),
        grid=(bsz, seq_len // tl),
        in_specs=[pl.BlockSpec((tl, n), lambda b, m: (m, 0)),
                  pl.BlockSpec((1, n, width), lambda b, m: (b, 0, 0)),
                  tile, tile,
                  _resident((1, width), lambda b, m: (0, 0))],
        out_specs=tile,
        out_shape=jax.ShapeDtypeStruct((bsz, seq_len, width), BF16),
        compiler_params=_params("parallel", "parallel"),
        name="hyena_inv_dft",
    )(gm, yhat, u, x0c, skip.astype(F32).reshape(1, width))


def _sgu_kernel(zu_ref, zv_ref, g_ref, b_ref, w_ref, sb_ref, o_ref):
    tokens, width = zu_ref.shape[1], zu_ref.shape[2]
    u = _gelu(zu_ref[0].astype(F32))
    v = _layer_norm(_gelu(zv_ref[0].astype(F32)), g_ref[...], b_ref[...]).astype(BF16)
    lane = lax.broadcasted_iota(jnp.int32, (SG_CHUNK, LANES), 1)
    first = lane < LANES // 2
    for c in range(tokens // SG_CHUNK):
        rows = slice(c * SG_CHUNK, (c + 1) * SG_CHUNK)
        for p in range(width // LANES):
            cols = slice(p * LANES, (p + 1) * LANES)
            r = jnp.dot(w_ref[p], v[rows, cols], preferred_element_type=F32)
            sv = jnp.where(first, r[:SG_CHUNK], r[SG_CHUNK:]) + sb_ref[:, cols]
            o_ref[0, rows, cols] = (u[rows, cols] * sv).astype(o_ref.dtype)


def _spatial_gating(p3, ln_g, ln_b, sg_w, sg_b, tokens=512):
    bsz, seq_len, _ = p3.shape
    groups = sg_w.shape[0]
    width = ln_g.shape[0]
    gw = width // groups
    assert 2 * gw == LANES and sg_w.shape[1] == SG_CHUNK
    wst = sg_w.astype(BF16).reshape(groups // 2, 2 * SG_CHUNK, SG_CHUNK)
    sb = jnp.repeat(sg_b.astype(F32).T, gw, axis=1)
    blk = (1, tokens, width)
    return pl.pallas_call(
        _sgu_kernel,
        grid=(bsz, seq_len // tokens),
        in_specs=[pl.BlockSpec(blk, lambda b, i: (b, i, 0)),
                  pl.BlockSpec(blk, lambda b, i: (b, i, 1)),
                  _resident((1, width), lambda b, i: (0, 0)),
                  _resident((1, width), lambda b, i: (0, 0)),
                  _resident((groups // 2, 2 * SG_CHUNK, SG_CHUNK), lambda b, i: (0, 0, 0)),
                  _resident((SG_CHUNK, width), lambda b, i: (0, 0))],
        out_specs=pl.BlockSpec(blk, lambda b, i: (b, i, 0)),
        out_shape=jax.ShapeDtypeStruct((bsz, seq_len, width), BF16),
        compiler_params=_params("parallel", "parallel"),
        name="spatial_gating",
    )(p3, p3, ln_g.astype(F32).reshape(1, width), ln_b.astype(F32).reshape(1, width), wst, sb)


CV_PAD = 16
CV_ROWS = 64


def _conf_kernel(za_ref, zg_ref, w_ref, wb_ref, g_ref, b_ref, o_ref, apad_ref):
    seq_len, width = za_ref.shape[1], za_ref.shape[2]
    fill_rows = 256
    apad_ref[0:CV_PAD, :] = jnp.zeros((CV_PAD, width), F32)
    apad_ref[seq_len + CV_PAD:seq_len + 2 * CV_PAD, :] = jnp.zeros((CV_PAD, width), F32)

    def fill(i, carry):
        r0 = pl.multiple_of(i * fill_rows, fill_rows)
        a = za_ref[0, pl.ds(r0, fill_rows), :].astype(F32)
        g = zg_ref[0, pl.ds(r0, fill_rows), :].astype(F32)
        apad_ref[pl.ds(r0 + CV_PAD, fill_rows), :] = a * jax.nn.sigmoid(g)
        return carry

    lax.fori_loop(0, seq_len // fill_rows, fill, 0)

    def conv(i, carry):
        r0 = pl.multiple_of(i * CV_ROWS, CV_ROWS)
        off0 = CV_PAD - CV_KERNEL // 2
        win = CV_ROWS + SUBLANES
        acc = jnp.zeros((CV_ROWS, width), F32)
        for s in range(SUBLANES):
            part = None
            for q in range((off0 + CV_KERNEL - 1) // SUBLANES + 1):
                j = SUBLANES * q + s - off0
                if 0 <= j < CV_KERNEL:
                    term = w_ref[j:j + 1, :] * apad_ref[pl.ds(r0 + SUBLANES * q, win), :]
                    part = term if part is None else part + term
            if s:
                part = pltpu.roll(part, win - s, 0)
            acc = acc + part[:CV_ROWS]
        y = _layer_norm(acc + wb_ref[...], g_ref[...], b_ref[...])
        o_ref[0, pl.ds(r0, CV_ROWS), :] = (y * jax.nn.sigmoid(y)).astype(o_ref.dtype)
        return carry

    lax.fori_loop(0, seq_len // CV_ROWS, conv, 0)


def _conformer_conv(p3, lane_block, dw_w, dw_b, ln_g, ln_b):
    bsz, seq_len, _ = p3.shape
    width = dw_w.shape[1]
    blk = (1, seq_len, width)
    vec = lambda a: a.astype(F32).reshape(1, width)
    return pl.pallas_call(
        _conf_kernel,
        grid=(bsz,),
        in_specs=[pl.BlockSpec(blk, lambda b: (b, 0, lane_block)),
                  pl.BlockSpec(blk, lambda b: (b, 0, lane_block + 1)),
                  _resident((CV_KERNEL, width), lambda b: (0, 0)),
                  _resident((1, width), lambda b: (0, 0)),
                  _resident((1, width), lambda b: (0, 0)),
                  _resident((1, width), lambda b: (0, 0))],
        out_specs=pl.BlockSpec(blk, lambda b: (b, 0, 0)),
        out_shape=jax.ShapeDtypeStruct((bsz, seq_len, width), BF16),
        scratch_shapes=[pltpu.VMEM((seq_len + 2 * CV_PAD, width), F32)],
        compiler_params=_params("parallel"),
        name="conformer_conv",
    )(p3, p3, dw_w.astype(F32), vec(dw_b), vec(ln_g), vec(ln_b))


def _out_mlp_kernel(x_ref, a_ref, b_ref, woa_ref, wob_ref, g_ref, w1_ref, w2_ref, gf_ref, o_ref,
                    *, hidden_chunk, final):
    x1 = (x_ref[...]
          + jnp.dot(a_ref[...], woa_ref[...], preferred_element_type=F32)
          + jnp.dot(b_ref[...], wob_ref[...], preferred_element_type=F32))
    hn = _rms(x1, g_ref[...]).astype(BF16)
    mlp = None
    for c in range(w1_ref.shape[1] // hidden_chunk):
        cols = slice(c * hidden_chunk, (c + 1) * hidden_chunk)
        t = jnp.maximum(jnp.dot(hn, w1_ref[:, cols], preferred_element_type=F32), 0.0)
        y = jnp.dot((t * t).astype(BF16), w2_ref[cols, :], preferred_element_type=F32)
        mlp = y if mlp is None else mlp + y
    acc = x1 + mlp
    if final:
        acc = _rms(acc, gf_ref[...])
    o_ref[...] = acc


def _out_mlp(x2, a, b, w_out, g, w1, w2, gf, final, tm=512, hidden_chunk=1024):
    m, d = x2.shape
    half = a.shape[1]
    hidden = w1.shape[1]
    return pl.pallas_call(
        functools.partial(_out_mlp_kernel, hidden_chunk=hidden_chunk, final=final),
        grid=(m // tm,),
        in_specs=[pl.BlockSpec((tm, d), lambda i: (i, 0)),
                  pl.BlockSpec((tm, half), lambda i: (i, 0)),
                  pl.BlockSpec((tm, half), lambda i: (i, 0)),
                  _resident((half, d), lambda i: (0, 0)),
                  _resident((half, d), lambda i: (1, 0)),
                  _resident((1, d), lambda i: (0, 0)),
                  _resident((d, hidden), lambda i: (0, 0)),
                  _resident((hidden, d), lambda i: (0, 0)),
                  _resident((1, d), lambda i: (0, 0))],
        out_specs=pl.BlockSpec((tm, d), lambda i: (i, 0)),
        out_shape=jax.ShapeDtypeStruct((m, d), F32),
        compiler_params=_params("parallel"),
        name="out_mlp",
    )(x2, a, b, w_out, w_out, g.reshape(1, d), w1, w2, gf.reshape(1, d))


def kernel(x, norm_g, ev_w_in, ev_rpb, hy_conv_w, hy_conv_b, hy_w1, hy_b1, hy_w2, hy_b2, hy_w3, hy_b3, hy_freq, hy_w_out, hy_skip, ev_w_out, od_w_in, sg_ln_g, sg_ln_b, sg_w, sg_b, cv_dw_w, cv_dw_b, cv_ln_g, cv_ln_b, od_w_out, mlp_w1, mlp_w2, final_g):
    bsz, seq_len, d = x.shape
    depth = norm_g.shape[0]
    m = bsz * seq_len
    x2 = x.astype(F32).reshape(m, d)
    na_width = ev_rpb.shape[1] * NA_HEAD_DIM
    for i in range(depth):
        j = i // 2
        if i % 2 == 0:
            p3 = _norm_proj(x2, norm_g[i, 0], ev_w_in[j].astype(BF16)).reshape(bsz, seq_len, -1)
            mix_a = _neighbourhood_attention(p3, ev_rpb[j])
            u, x0c = _hyena_pre(p3, 3 * na_width, hy_conv_w[j], hy_conv_b[j])
            h, ss = _hyena_filters(seq_len, hy_w1[j], hy_b1[j], hy_w2[j], hy_b2[j], hy_w3[j], hy_b3[j],
                                   hy_freq[j], hy_w_out[j])
            mix_b = _hyena_long_conv(u, x0c, h, ss, hy_skip[j])
            w_out = ev_w_out[j]
        else:
            p3 = _norm_proj(x2, norm_g[i, 0], od_w_in[j].astype(BF16)).reshape(bsz, seq_len, -1)
            mix_a = _spatial_gating(p3, sg_ln_g[j], sg_ln_b[j], sg_w[j], sg_b[j])
            mix_b = _conformer_conv(p3, 2, cv_dw_w[j], cv_dw_b[j], cv_ln_g[j], cv_ln_b[j])
            w_out = od_w_out[j]
        x2 = _out_mlp(x2, mix_a.reshape(m, -1), mix_b.reshape(m, -1), w_out.astype(BF16), norm_g[i, 1],
                      mlp_w1[i].astype(BF16), mlp_w2[i].astype(BF16), final_g, final=(i == depth - 1))
    return x2.reshape(bsz, seq_len, d).astype(x.dtype)
```

```python
import functools
import math

import numpy as np
import jax
import jax.numpy as jnp
from jax import lax
from jax.experimental import pallas as pl
from jax.experimental.pallas import tpu as pltpu

F32 = jnp.float32
BF16 = jnp.bfloat16
EPS = 1e-6
MASK_VALUE = -1e30

GRID_W = 64
NA_HEAD_DIM = 64
NA_WIN_R = 8
NA_WIN_C = 16
HY_EMB = 33
HY_TARGET = 1e-2
HY_FAST_PCT = 0.3
HY_SLOW_PCT = 1.5
SG_CHUNK = 128
CV_KERNEL = 31

LANES = 128
SUBLANES = 8
VMEM_LIMIT = 56 * 1024 * 1024


def _params(*sem):
    return pltpu.CompilerParams(dimension_semantics=sem, vmem_limit_bytes=VMEM_LIMIT)


def _resident(shape, index_map):
    return pl.BlockSpec(shape, index_map, pipeline_mode=pl.Buffered(1))


def _rms(x, g):
    return x * lax.rsqrt(jnp.mean(x * x, axis=-1, keepdims=True) + EPS) * g


def _layer_norm(x, g, b):
    xc = x - jnp.mean(x, axis=-1, keepdims=True)
    return xc * lax.rsqrt(jnp.mean(xc * xc, axis=-1, keepdims=True) + EPS) * g + b


def _gelu(x):
    return x * (lax.erf(x / np.float32(math.sqrt(2.0))) + 1.0) / 2.0


def _norm_proj_kernel(x_ref, g_ref, w_ref, o_ref):
    hn = _rms(x_ref[...], g_ref[...]).astype(BF16)
    o_ref[...] = jnp.dot(hn, w_ref[...], preferred_element_type=F32).astype(o_ref.dtype)


def _norm_proj(x2, g, w, tm=512):
    m, d = x2.shape
    n = w.shape[1]
    return pl.pallas_call(
        _norm_proj_kernel,
        grid=(m // tm,),
        in_specs=[pl.BlockSpec((tm, d), lambda i: (i, 0)),
                  _resident((1, d), lambda i: (0, 0)),
                  _resident((d, n), lambda i: (0, 0))],
        out_specs=pl.BlockSpec((tm, n), lambda i: (i, 0)),
        out_shape=jax.ShapeDtypeStruct((m, n), BF16),
        compiler_params=_params("parallel"),
        name="norm_proj",
    )(x2, g.reshape(1, d), w)


NA_UNROLL = 8


def _na_kernel(q_ref, k_ref, v_ref, tb_ref, o_ref):
    seq_len = q_ref.shape[1]
    rows = seq_len // GRID_W
    nkeys = NA_WIN_R * GRID_W
    lane = lax.broadcasted_iota(jnp.int32, (GRID_W, LANES), 1)
    first = lane < NA_HEAD_DIM

    def body(i, carry):
        q0s, k0s, cases, scores = [], [], [], []
        for u in range(NA_UNROLL):
            r = i * NA_UNROLL + u
            rs = jnp.clip(r - NA_WIN_R // 2, 0, rows - NA_WIN_R)
            cases.append(r - rs)
            q0s.append(pl.multiple_of(r * GRID_W, GRID_W))
            k0s.append(pl.multiple_of(rs * GRID_W, GRID_W))
            q2 = q_ref[0, pl.ds(q0s[u], GRID_W), :] * BF16(NA_HEAD_DIM ** -0.5)
            k2 = k_ref[0, pl.ds(k0s[u], nkeys), :]
            zero = jnp.zeros_like(q2)
            qs = jnp.concatenate([jnp.where(first, q2, zero), jnp.where(first, zero, q2)], axis=0)
            scores.append(lax.dot_general(qs, k2, (((1,), (1,)), ((), ())), preferred_element_type=F32))
        probs, dens = [], []
        for u in range(NA_UNROLL):
            s = scores[u] + tb_ref[0, cases[u]]
            e = jnp.exp(s - jnp.max(s, axis=-1, keepdims=True))
            dens.append(jnp.sum(e, axis=-1, keepdims=True))
            probs.append(e.astype(BF16))
        outs = [jnp.dot(probs[u], v_ref[0, pl.ds(k0s[u], nkeys), :], preferred_element_type=F32)
                for u in range(NA_UNROLL)]
        for u in range(NA_UNROLL):
            o = outs[u] / dens[u]
            o_ref[0, pl.ds(q0s[u], GRID_W), :] = jnp.where(first, o[:GRID_W], o[GRID_W:]).astype(o_ref.dtype)
        return carry

    lax.fori_loop(0, rows // NA_UNROLL, body, 0)


def _na_bias_table(rpb):
    heads, nrow, ncol = rpb.shape
    qc = np.arange(GRID_W)[:, None]
    kc = np.arange(GRID_W)[None, :]
    cs = np.clip(qc - NA_WIN_C // 2, 0, GRID_W - NA_WIN_C)
    inside = (kc >= cs) & (kc < cs + NA_WIN_C)
    col_off = kc - qc + (NA_WIN_C - 1)
    onehot = (col_off[None] == np.arange(ncol)[:, None, None]) & inside[None]
    expand = jnp.asarray(onehot.reshape(ncol, GRID_W * GRID_W), F32)
    t = jnp.dot(rpb.astype(F32).reshape(heads * nrow, ncol), expand, precision=lax.Precision.HIGHEST)
    t = jnp.where(inside[None, None], t.reshape(heads, nrow, GRID_W, GRID_W), MASK_VALUE)
    t = jnp.stack([t[:, NA_WIN_R - 1 - c:2 * NA_WIN_R - 1 - c] for c in range(NA_WIN_R)], axis=1)
    t = t.transpose(0, 1, 3, 2, 4).reshape(heads // 2, 2, NA_WIN_R, GRID_W, NA_WIN_R * GRID_W)
    return t.transpose(0, 2, 1, 3, 4).reshape(heads // 2, NA_WIN_R, 2 * GRID_W, NA_WIN_R * GRID_W)


def _neighbourhood_attention(p3, rpb):
    bsz, seq_len, _ = p3.shape
    heads = rpb.shape[0]
    pairs = heads // 2
    tb = _na_bias_table(rpb)
    blk = (1, seq_len, LANES)
    return pl.pallas_call(
        _na_kernel,
        grid=(bsz, pairs),
        in_specs=[pl.BlockSpec(blk, lambda b, j: (b, 0, j)),
                  pl.BlockSpec(blk, lambda b, j: (b, 0, pairs + j)),
                  pl.BlockSpec(blk, lambda b, j: (b, 0, 2 * pairs + j)),
                  pl.BlockSpec((1, NA_WIN_R, 2 * GRID_W, NA_WIN_R * GRID_W), lambda b, j: (j, 0, 0, 0))],
        out_specs=pl.BlockSpec(blk, lambda b, j: (b, 0, j)),
        out_shape=jax.ShapeDtypeStruct((bsz, seq_len, pairs * LANES), BF16),
        compiler_params=_params("parallel", "parallel"),
        name="na_attention",
    )(p3, p3, p3, tb)


def _hy_filter_kernel(frq_ref, w1_ref, b1_ref, w2_ref, b2_ref, w3_ref, b3_ref, fq_ref, fo_ref,
                      dl_ref, h_ref, ss_ref, *, seq_len):
    i = pl.program_id(0)
    tl = h_ref.shape[0]
    hp = lax.Precision.HIGHEST
    pos = (lax.broadcasted_iota(jnp.int32, (tl, LANES), 0) + i * tl).astype(F32)
    lane = lax.broadcasted_iota(jnp.int32, (tl, LANES), 1)
    bands = (HY_EMB - 1) // 2
    t = pos / np.float32(seq_len - 1)
    ang = (np.float32(2.0 * math.pi) * pos / np.float32(seq_len)) * frq_ref[...]
    z = jnp.where(lane == 0, t,
                  jnp.where(lane <= bands, jnp.cos(ang),
                            jnp.where(lane <= 2 * bands, -jnp.sin(ang), 0.0)))
    fq = fq_ref[...]
    hid = jnp.sin(fq * (jnp.dot(z, w1_ref[...], precision=hp, preferred_element_type=F32) + b1_ref[...]))
    hid = jnp.sin(fq * (jnp.dot(hid, w2_ref[...], precision=hp, preferred_element_type=F32) + b2_ref[...]))
    hid = jnp.sin(fq * (jnp.dot(hid, w3_ref[...], precision=hp, preferred_element_type=F32) + b3_ref[...]))
    h = jnp.dot(hid, fo_ref[...], precision=hp, preferred_element_type=F32)
    h = h * jnp.exp(-t[:, :1] * dl_ref[...])
    h_ref[...] = h.astype(h_ref.dtype)

    @pl.when(i == 0)
    def _():
        ss_ref[...] = jnp.zeros_like(ss_ref)

    ss_ref[...] += jnp.sum(h * h, axis=0, keepdims=True)


def _pad2(a, rows, cols):
    return jnp.pad(a.astype(F32), ((0, rows - a.shape[0]), (0, cols - a.shape[1])))


def _hyena_filters(seq_len, w1, b1, w2, b2, w3, b3, freq, f_out, tl=512):
    ffn = w1.shape[1]
    width2 = f_out.shape[1]
    bands = (HY_EMB - 1) // 2
    fr = np.linspace(1e-4, bands - 1, bands, dtype=np.float32)
    frq = np.zeros((1, LANES), np.float32)
    frq[0, 1:1 + bands] = fr
    frq[0, 1 + bands:1 + 2 * bands] = fr
    deltas = np.abs(np.linspace(math.log(HY_TARGET) / HY_SLOW_PCT, math.log(HY_TARGET) / HY_FAST_PCT,
                                width2 // 2, dtype=np.float32))
    dl = np.tile(deltas, 2)[None, :]
    row = lambda a: _pad2(a.reshape(1, -1), 1, LANES)
    small = lambda shape: _resident(shape, lambda i: (0, 0))
    return pl.pallas_call(
        functools.partial(_hy_filter_kernel, seq_len=seq_len),
        grid=(seq_len // tl,),
        in_specs=[small((1, LANES)), small((LANES, LANES)), small((1, LANES)), small((LANES, LANES)),
                  small((1, LANES)), small((LANES, LANES)), small((1, LANES)), small((1, LANES)),
                  small((LANES, width2)), small((1, width2))],
        out_specs=[pl.BlockSpec((tl, width2), lambda i: (i, 0)),
                   pl.BlockSpec((1, width2), lambda i: (0, 0))],
        out_shape=[jax.ShapeDtypeStruct((seq_len, width2), BF16),
                   jax.ShapeDtypeStruct((1, width2), F32)],
        compiler_params=_params("arbitrary"),
        name="hyena_filters",
    )(jnp.asarray(frq), _pad2(w1, LANES, LANES), row(b1), _pad2(w2, LANES, LANES), row(b2),
      _pad2(w3, LANES, LANES), row(b3), row(freq), _pad2(f_out, LANES, width2), jnp.asarray(dl))


def _dft_matrix(seq_len, tile_bins):
    n = 2 * seq_len
    k = jnp.arange(seq_len, dtype=jnp.int32)
    t = jnp.arange(seq_len, dtype=jnp.int32)
    phase = ((2 * k[:, None] + 1) * t[None, :]) % (2 * n)
    ang = phase.astype(F32) * np.float32(2.0 * math.pi / (2 * n))
    c = jnp.cos(ang).reshape(seq_len // tile_bins, 1, tile_bins, seq_len)
    s = (-jnp.sin(ang)).reshape(seq_len // tile_bins, 1, tile_bins, seq_len)
    return jnp.concatenate([c, s], axis=1).reshape(n, seq_len).astype(BF16)


def _hy_pre_kernel(x0_ref, x1_ref, v_ref, w_ref, b_ref, u_ref, x0c_ref):
    seq_len = x0_ref.shape[1]
    row = lax.broadcasted_iota(jnp.int32, (seq_len, LANES), 0)

    def conv(z_ref, i):
        z = z_ref[0].astype(F32)
        zm = jnp.where(row == 0, 0.0, pltpu.roll(z, 1, 0))
        zp = jnp.where(row == seq_len - 1, 0.0, pltpu.roll(z, seq_len - 1, 0))
        w = w_ref[i]
        return w[0:1] * zm + w[1:2] * z + w[2:3] * zp + b_ref[i]

    x0c_ref[0] = conv(x0_ref, 0).astype(x0c_ref.dtype)
    u_ref[0] = (conv(v_ref, 2) * conv(x1_ref, 1)).astype(u_ref.dtype)


def _hyena_pre(p3, lane0, conv_w, conv_b):
    bsz, seq_len, _ = p3.shape
    width = conv_w.shape[1] // 3
    nt = width // LANES
    off = lane0 // LANES
    w3 = conv_w.astype(F32).reshape(conv_w.shape[0], 3, width).transpose(1, 0, 2)
    b3 = conv_b.astype(F32).reshape(3, 1, width)
    blk = (1, seq_len, LANES)
    out = jax.ShapeDtypeStruct((bsz, seq_len, width), BF16)
    return pl.pallas_call(
        _hy_pre_kernel,
        grid=(bsz, nt),
        in_specs=[pl.BlockSpec(blk, lambda b, j: (b, 0, off + j)),
                  pl.BlockSpec(blk, lambda b, j: (b, 0, off + nt + j)),
                  pl.BlockSpec(blk, lambda b, j: (b, 0, off + 2 * nt + j)),
                  pl.BlockSpec((3, conv_w.shape[0], LANES), lambda b, j: (0, 0, j)),
                  pl.BlockSpec((3, 1, LANES), lambda b, j: (0, 0, j))],
        out_specs=[pl.BlockSpec(blk, lambda b, j: (b, 0, j)),
                   pl.BlockSpec(blk, lambda b, j: (b, 0, j))],
        out_shape=[out, out],
        compiler_params=_params("parallel", "parallel"),
        name="hyena_pre",
    )(p3, p3, p3, w3, b3)


def _hy_filter_dft_kernel(f_ref, h_ref, ss_ref, o_ref):
    acc = jnp.dot(f_ref[...], h_ref[...], preferred_element_type=F32)
    tb = acc.shape[0] // 2
    hy = acc.shape[1] // 2
    ss = ss_ref[...]
    scale = lax.rsqrt(ss[:, :hy] + ss[:, hy:])
    o_ref[:tb, :] = (acc[:tb, :hy] + acc[:tb, hy:]) * scale
    o_ref[tb:, :] = (acc[tb:, :hy] - acc[tb:, hy:]) * scale


def _hy_fwd_kernel(f_ref, u_ref, h_ref, y_ref):
    acc = jnp.dot(f_ref[...], u_ref[0], preferred_element_type=F32)
    tb = acc.shape[0] // 2
    re, im = acc[:tb], acc[tb:]
    hre, him = h_ref[:tb, :], h_ref[tb:, :]
    y_ref[0, :tb, :] = (re * hre - im * him).astype(y_ref.dtype)
    y_ref[0, tb:, :] = (re * him + im * hre).astype(y_ref.dtype)


def _hy_inv_kernel(g_ref, y_ref, u_ref, x0_ref, skip_ref, o_ref, *, scale):
    y = jnp.dot(g_ref[...], y_ref[0], preferred_element_type=F32) * np.float32(scale)
    y = y + u_ref[0].astype(F32) * skip_ref[...]
    o_ref[0] = (y * x0_ref[0].astype(F32)).astype(o_ref.dtype)


def _hyena_long_conv(u, x0c, h, ss, skip, tile_bins=256, tl=512):
    bsz, seq_len, width = u.shape
    n = 2 * seq_len
    tm = 2 * tile_bins
    fm = _dft_matrix(seq_len, tile_bins)
    gm = fm.T
    hhat = pl.pallas_call(
        _hy_filter_dft_kernel,
        grid=(n // tm,),
        in_specs=[pl.BlockSpec((tm, seq_len), lambda m: (m, 0)),
                  _resident((seq_len, 2 * width), lambda m: (0, 0)),
                  _resident((1, 2 * width), lambda m: (0, 0))],
        out_specs=pl.BlockSpec((tm, width), lambda m: (m, 0)),
        out_shape=jax.ShapeDtypeStruct((n, width), F32),
        compiler_params=_params("parallel"),
        name="hyena_filter_dft",
    )(fm, h, ss)
    yhat = pl.pallas_call(
        _hy_fwd_kernel,
        grid=(n // tm, bsz),
        in_specs=[pl.BlockSpec((tm, seq_len), lambda m, b: (m, 0)),
                  pl.BlockSpec((1, seq_len, width), lambda m, b: (b, 0, 0)),
                  pl.BlockSpec((tm, width), lambda m, b: (m, 0))],
        out_specs=pl.BlockSpec((1, tm, width), lambda m, b: (b, m, 0)),
        out_shape=jax.ShapeDtypeStruct((bsz, n, width), BF16),
        compiler_params=_params("parallel", "parallel"),
        name="hyena_fwd_dft",
    )(fm, u, hhat)
    tile = pl.BlockSpec((1, tl, width), lambda b, m: (b, m, 0))
    return pl.pallas_call(
        functools.partial(_hy_inv_kernel, scale=2.0 / n),
        grid=(bsz, seq_len // tl),
        in_specs=[pl.BlockSpec((tl, n), lambda b, m: (m, 0)),
                  pl.BlockSpec((1, n, width), lambda b, m: (b, 0, 0)),
                  tile, tile,
                  _resident((1, width), lambda b, m: (0, 0))],
        out_specs=tile,
        out_shape=jax.ShapeDtypeStruct((bsz, seq_len, width), BF16),
        compiler_params=_params("parallel", "parallel"),
        name="hyena_inv_dft",
    )(gm, yhat, u, x0c, skip.astype(F32).reshape(1, width))


def _sgu_kernel(zu_ref, zv_ref, g_ref, b_ref, w_ref, sb_ref, o_ref):
    tokens, width = zu_ref.shape[1], zu_ref.shape[2]
    u = _gelu(zu_ref[0].astype(F32))
    v = _layer_norm(_gelu(zv_ref[0].astype(F32)), g_ref[...], b_ref[...]).astype(BF16)
    lane = lax.broadcasted_iota(jnp.int32, (SG_CHUNK, LANES), 1)
    first = lane < LANES // 2
    for c in range(tokens // SG_CHUNK):
        rows = slice(c * SG_CHUNK, (c + 1) * SG_CHUNK)
        for p in range(width // LANES):
            cols = slice(p * LANES, (p + 1) * LANES)
            r = jnp.dot(w_ref[p], v[rows, cols], preferred_element_type=F32)
            sv = jnp.where(first, r[:SG_CHUNK], r[SG_CHUNK:]) + sb_ref[:, cols]
            o_ref[0, rows, cols] = (u[rows, cols] * sv).astype(o_ref.dtype)


def _spatial_gating(p3, ln_g, ln_b, sg_w, sg_b, tokens=512):
    bsz, seq_len, _ = p3.shape
    groups = sg_w.shape[0]
    width = ln_g.shape[0]
    gw = width // groups
    assert 2 * gw == LANES and sg_w.shape[1] == SG_CHUNK
    wst = sg_w.astype(BF16).reshape(groups // 2, 2 * SG_CHUNK, SG_CHUNK)
    sb = jnp.repeat(sg_b.astype(F32).T, gw, axis=1)
    blk = (1, tokens, width)
    return pl.pallas_call(
        _sgu_kernel,
        grid=(bsz, seq_len // tokens),
        in_specs=[pl.BlockSpec(blk, lambda b, i: (b, i, 0)),
                  pl.BlockSpec(blk, lambda b, i: (b, i, 1)),
                  _resident((1, width), lambda b, i: (0, 0)),
                  _resident((1, width), lambda b, i: (0, 0)),
                  _resident((groups // 2, 2 * SG_CHUNK, SG_CHUNK), lambda b, i: (0, 0, 0)),
                  _resident((SG_CHUNK, width), lambda b, i: (0, 0))],
        out_specs=pl.BlockSpec(blk, lambda b, i: (b, i, 0)),
        out_shape=jax.ShapeDtypeStruct((bsz, seq_len, width), BF16),
        compiler_params=_params("parallel", "parallel"),
        name="spatial_gating",
    )(p3, p3, ln_g.astype(F32).reshape(1, width), ln_b.astype(F32).reshape(1, width), wst, sb)


CV_PAD = 16
CV_ROWS = 64


def _conf_kernel(za_ref, zg_ref, w_ref, wb_ref, g_ref, b_ref, o_ref, apad_ref):
    seq_len, width = za_ref.shape[1], za_ref.shape[2]
    fill_rows = 256
    apad_ref[0:CV_PAD, :] = jnp.zeros((CV_PAD, width), F32)
    apad_ref[seq_len + CV_PAD:seq_len + 2 * CV_PAD, :] = jnp.zeros((CV_PAD, width), F32)

    def fill(i, carry):
        r0 = pl.multiple_of(i * fill_rows, fill_rows)
        a = za_ref[0, pl.ds(r0, fill_rows), :].astype(F32)
        g = zg_ref[0, pl.ds(r0, fill_rows), :].astype(F32)
        apad_ref[pl.ds(r0 + CV_PAD, fill_rows), :] = a * jax.nn.sigmoid(g)
        return carry

    lax.fori_loop(0, seq_len // fill_rows, fill, 0)

    def conv(i, carry):
        r0 = pl.multiple_of(i * CV_ROWS, CV_ROWS)
        off0 = CV_PAD - CV_KERNEL // 2
        win = CV_ROWS + SUBLANES
        acc = jnp.zeros((CV_ROWS, width), F32)
        for s in range(SUBLANES):
            part = None
            for q in range((off0 + CV_KERNEL - 1) // SUBLANES + 1):
                j = SUBLANES * q + s - off0
                if 0 <= j < CV_KERNEL:
                    term = w_ref[j:j + 1, :] * apad_ref[pl.ds(r0 + SUBLANES * q, win), :]
                    part = term if part is None else part + term
            if s:
                part = pltpu.roll(part, win - s, 0)
            acc = acc + part[:CV_ROWS]
        y = _layer_norm(acc + wb_ref[...], g_ref[...], b_ref[...])
        o_ref[0, pl.ds(r0, CV_ROWS), :] = (y * jax.nn.sigmoid(y)).astype(o_ref.dtype)
        return carry

    lax.fori_loop(0, seq_len // CV_ROWS, conv, 0)


def _conformer_conv(p3, lane_block, dw_w, dw_b, ln_g, ln_b):
    bsz, seq_len, _ = p3.shape
    width = dw_w.shape[1]
    blk = (1, seq_len, width)
    vec = lambda a: a.astype(F32).reshape(1, width)
    return pl.pallas_call(
        _conf_kernel,
        grid=(bsz,),
        in_specs=[pl.BlockSpec(blk, lambda b: (b, 0, lane_block)),
                  pl.BlockSpec(blk, lambda b: (b, 0, lane_block + 1)),
                  _resident((CV_KERNEL, width), lambda b: (0, 0)),
                  _resident((1, width), lambda b: (0, 0)),
                  _resident((1, width), lambda b: (0, 0)),
                  _resident((1, width), lambda b: (0, 0))],
        out_specs=pl.BlockSpec(blk, lambda b: (b, 0, 0)),
        out_shape=jax.ShapeDtypeStruct((bsz, seq_len, width), BF16),
        scratch_shapes=[pltpu.VMEM((seq_len + 2 * CV_PAD, width), F32)],
        compiler_params=_params("parallel"),
        name="conformer_conv",
    )(p3, p3, dw_w.astype(F32), vec(dw_b), vec(ln_g), vec(ln_b))


def _out_mlp_kernel(x_ref, a_ref, b_ref, woa_ref, wob_ref, g_ref, w1_ref, w2_ref, gf_ref, o_ref,
                    *, hidden_chunk, final):
    x1 = (x_ref[...]
          + jnp.dot(a_ref[...], woa_ref[...], preferred_element_type=F32)
          + jnp.dot(b_ref[...], wob_ref[...], preferred_element_type=F32))
    hn = _rms(x1, g_ref[...]).astype(BF16)
    mlp = None
    for c in range(w1_ref.shape[1] // hidden_chunk):
        cols = slice(c * hidden_chunk, (c + 1) * hidden_chunk)
        t = jnp.maximum(jnp.dot(hn, w1_ref[:, cols], preferred_element_type=F32), 0.0)
        y = jnp.dot((t * t).astype(BF16), w2_ref[cols, :], preferred_element_type=F32)
        mlp = y if mlp is None else mlp + y
    acc = x1 + mlp
    if final:
        acc = _rms(acc, gf_ref[...])
    o_ref[...] = acc


def _out_mlp(x2, a, b, w_out, g, w1, w2, gf, final, tm=512, hidden_chunk=1024):
    m, d = x2.shape
    half = a.shape[1]
    hidden = w1.shape[1]
    return pl.pallas_call(
        functools.partial(_out_mlp_kernel, hidden_chunk=hidden_chunk, final=final),
        grid=(m // tm,),
        in_specs=[pl.BlockSpec((tm, d), lambda i: (i, 0)),
                  pl.BlockSpec((tm, half), lambda i: (i, 0)),
                  pl.BlockSpec((tm, half), lambda i: (i, 0)),
                  _resident((half, d), lambda i: (0, 0)),
                  _resident((half, d), lambda i: (1, 0)),
                  _resident((1, d), lambda i: (0, 0)),
                  _resident((d, hidden), lambda i: (0, 0)),
                  _resident((hidden, d), lambda i: (0, 0)),
                  _resident((1, d), lambda i: (0, 0))],
        out_specs=pl.BlockSpec((tm, d), lambda i: (i, 0)),
        out_shape=jax.ShapeDtypeStruct((m, d), F32),
        compiler_params=_params("parallel"),
        name="out_mlp",
    )(x2, a, b, w_out, w_out, g.reshape(1, d), w1, w2, gf.reshape(1, d))


def kernel(x, norm_g, ev_w_in, ev_rpb, hy_conv_w, hy_conv_b, hy_w1, hy_b1, hy_w2, hy_b2, hy_w3, hy_b3, hy_freq, hy_w_out, hy_skip, ev_w_out, od_w_in, sg_ln_g, sg_ln_b, sg_w, sg_b, cv_dw_w, cv_dw_b, cv_ln_g, cv_ln_b, od_w_out, mlp_w1, mlp_w2, final_g):
    bsz, seq_len, d = x.shape
    depth = norm_g.shape[0]
    m = bsz * seq_len
    x2 = x.astype(F32).reshape(m, d)
    na_width = ev_rpb.shape[1] * NA_HEAD_DIM
    for i in range(depth):
        j = i // 2
        if i % 2 == 0:
            p3 = _norm_proj(x2, norm_g[i, 0], ev_w_in[j].astype(BF16)).reshape(bsz, seq_len, -1)
            mix_a = _neighbourhood_attention(p3, ev_rpb[j])
            u, x0c = _hyena_pre(p3, 3 * na_width, hy_conv_w[j], hy_conv_b[j])
            h, ss = _hyena_filters(seq_len, hy_w1[j], hy_b1[j], hy_w2[j], hy_b2[j], hy_w3[j], hy_b3[j],
                                   hy_freq[j], hy_w_out[j])
            mix_b = _hyena_long_conv(u, x0c, h, ss, hy_skip[j])
            w_out = ev_w_out[j]
        else:
            p3 = _norm_proj(x2, norm_g[i, 0], od_w_in[j].astype(BF16)).reshape(bsz, seq_len, -1)
            mix_a = _spatial_gating(p3, sg_ln_g[j], sg_ln_b[j], sg_w[j], sg_b[j])
            mix_b = _conformer_conv(p3, 2, cv_dw_w[j], cv_dw_b[j], cv_ln_g[j], cv_ln_b[j])
            w_out = od_w_out[j]
        x2 = _out_mlp(x2, mix_a.reshape(m, -1), mix_b.reshape(m, -1), w_out.astype(BF16), norm_g[i, 1],
                      mlp_w1[i].astype(BF16), mlp_w2[i].astype(BF16), final_g, final=(i == depth - 1))
    return x2.reshape(bsz, seq_len, d).astype(x.dtype)
```

```python
import functools
import math

import numpy as np
import jax
import jax.numpy as jnp
from jax import lax
from jax.experimental import pallas as pl
from jax.experimental.pallas import tpu as pltpu

F32 = jnp.float32
BF16 = jnp.bfloat16
EPS = 1e-6
MASK_VALUE = -1e30

GRID_W = 64
NA_HEAD_DIM = 64
NA_WIN_R = 8
NA_WIN_C = 16
HY_EMB = 33
HY_TARGET = 1e-2
HY_FAST_PCT = 0.3
HY_SLOW_PCT = 1.5
SG_CHUNK = 128
CV_KERNEL = 31

LANES = 128
SUBLANES = 8
VMEM_LIMIT = 56 * 1024 * 1024


def _params(*sem):
    return pltpu.CompilerParams(dimension_semantics=sem, vmem_limit_bytes=VMEM_LIMIT)


def _resident(shape, index_map):
    return pl.BlockSpec(shape, index_map, pipeline_mode=pl.Buffered(1))


def _rms(x, g):
    return x * lax.rsqrt(jnp.mean(x * x, axis=-1, keepdims=True) + EPS) * g


def _layer_norm(x, g, b):
    xc = x - jnp.mean(x, axis=-1, keepdims=True)
    return xc * lax.rsqrt(jnp.mean(xc * xc, axis=-1, keepdims=True) + EPS) * g + b


def _gelu(x):
    return x * (lax.erf(x / np.float32(math.sqrt(2.0))) + 1.0) / 2.0


def _norm_proj_kernel(x_ref, g_ref, w_ref, o_ref):
    hn = _rms(x_ref[...], g_ref[...]).astype(BF16)
    o_ref[...] = jnp.dot(hn, w_ref[...], preferred_element_type=F32).astype(o_ref.dtype)


def _norm_proj(x2, g, w, tm=512):
    m, d = x2.shape
    n = w.shape[1]
    return pl.pallas_call(
        _norm_proj_kernel,
        grid=(m // tm,),
        in_specs=[pl.BlockSpec((tm, d), lambda i: (i, 0)),
                  _resident((1, d), lambda i: (0, 0)),
                  _resident((d, n), lambda i: (0, 0))],
        out_specs=pl.BlockSpec((tm, n), lambda i: (i, 0)),
        out_shape=jax.ShapeDtypeStruct((m, n), BF16),
        compiler_params=_params("parallel"),
        name="norm_proj",
    )(x2, g.reshape(1, d), w)


NA_UNROLL = 8


def _na_kernel(q_ref, k_ref, v_ref, tb_ref, o_ref):
    seq_len = q_ref.shape[1]
    rows = seq_len // GRID_W
    nkeys = NA_WIN_R * GRID_W
    lane = lax.broadcasted_iota(jnp.int32, (GRID_W, LANES), 1)
    first = lane < NA_HEAD_DIM

    def body(i, carry):
        q0s, k0s, cases, scores = [], [], [], []
        for u in range(NA_UNROLL):
            r = i * NA_UNROLL + u
            rs = jnp.clip(r - NA_WIN_R // 2, 0, rows - NA_WIN_R)
            cases.append(r - rs)
            q0s.append(pl.multiple_of(r * GRID_W, GRID_W))
            k0s.append(pl.multiple_of(rs * GRID_W, GRID_W))
            q2 = q_ref[0, pl.ds(q0s[u], GRID_W), :] * BF16(NA_HEAD_DIM ** -0.5)
            k2 = k_ref[0, pl.ds(k0s[u], nkeys), :]
            zero = jnp.zeros_like(q2)
            qs = jnp.concatenate([jnp.where(first, q2, zero), jnp.where(first, zero, q2)], axis=0)
            scores.append(lax.dot_general(qs, k2, (((1,), (1,)), ((), ())), preferred_element_type=F32))
        probs, dens = [], []
        for u in range(NA_UNROLL):
            s = scores[u] + tb_ref[0, cases[u]]
            e = jnp.exp(s - jnp.max(s, axis=-1, keepdims=True))
            dens.append(jnp.sum(e, axis=-1, keepdims=True))
            probs.append(e.astype(BF16))
        outs = [jnp.dot(probs[u], v_ref[0, pl.ds(k0s[u], nkeys), :], preferred_element_type=F32)
                for u in range(NA_UNROLL)]
        for u in range(NA_UNROLL):
            o = outs[u] / dens[u]
            o_ref[0, pl.ds(q0s[u], GRID_W), :] = jnp.where(first, o[:GRID_W], o[GRID_W:]).astype(o_ref.dtype)
        return carry

    lax.fori_loop(0, rows // NA_UNROLL, body, 0)


def _na_bias_table(rpb):
    heads, nrow, ncol = rpb.shape
    qc = np.arange(GRID_W)[:, None]
    kc = np.arange(GRID_W)[None, :]
    cs = np.clip(qc - NA_WIN_C // 2, 0, GRID_W - NA_WIN_C)
    inside = (kc >= cs) & (kc < cs + NA_WIN_C)
    col_off = kc - qc + (NA_WIN_C - 1)
    onehot = (col_off[None] == np.arange(ncol)[:, None, None]) & inside[None]
    expand = jnp.asarray(onehot.reshape(ncol, GRID_W * GRID_W), F32)
    t = jnp.dot(rpb.astype(F32).reshape(heads * nrow, ncol), expand, precision=lax.Precision.HIGHEST)
    t = jnp.where(inside[None, None], t.reshape(heads, nrow, GRID_W, GRID_W), MASK_VALUE)
    t = jnp.stack([t[:, NA_WIN_R - 1 - c:2 * NA_WIN_R - 1 - c] for c in range(NA_WIN_R)], axis=1)
    t = t.transpose(0, 1, 3, 2, 4).reshape(heads // 2, 2, NA_WIN_R, GRID_W, NA_WIN_R * GRID_W)
    return t.transpose(0, 2, 1, 3, 4).reshape(heads // 2, NA_WIN_R, 2 * GRID_W, NA_WIN_R * GRID_W)


def _neighbourhood_attention(p3, rpb):
    bsz, seq_len, _ = p3.shape
    heads = rpb.shape[0]
    pairs = heads // 2
    tb = _na_bias_table(rpb)
    blk = (1, seq_len, LANES)
    return pl.pallas_call(
        _na_kernel,
        grid=(bsz, pairs),
        in_specs=[pl.BlockSpec(blk, lambda b, j: (b, 0, j)),
                  pl.BlockSpec(blk, lambda b, j: (b, 0, pairs + j)),
                  pl.BlockSpec(blk, lambda b, j: (b, 0, 2 * pairs + j)),
                  pl.BlockSpec((1, NA_WIN_R, 2 * GRID_W, NA_WIN_R * GRID_W), lambda b, j: (j, 0, 0, 0))],
        out_specs=pl.BlockSpec(blk, lambda b, j: (b, 0, j)),
        out_shape=jax.ShapeDtypeStruct((bsz, seq_len, pairs * LANES), BF16),
        compiler_params=_params("parallel", "parallel"),
        name="na_attention",
    )(p3, p3, p3, tb)


def _hy_filter_kernel(frq_ref, w1_ref, b1_ref, w2_ref, b2_ref, w3_ref, b3_ref, fq_ref, fo_ref,
                      dl_ref, h_ref, ss_ref, *, seq_len):
    i = pl.program_id(0)
    tl = h_ref.shape[0]
    hp = lax.Precision.HIGHEST
    pos = (lax.broadcasted_iota(jnp.int32, (tl, LANES), 0) + i * tl).astype(F32)
    lane = lax.broadcasted_iota(jnp.int32, (tl, LANES), 1)
    bands = (HY_EMB - 1) // 2
    t = pos / np.float32(seq_len - 1)
    ang = (np.float32(2.0 * math.pi) * pos / np.float32(seq_len)) * frq_ref[...]
    z = jnp.where(lane == 0, t,
                  jnp.where(lane <= bands, jnp.cos(ang),
                            jnp.where(lane <= 2 * bands, -jnp.sin(ang), 0.0)))
    fq = fq_ref[...]
    hid = jnp.sin(fq * (jnp.dot(z, w1_ref[...], precision=hp, preferred_element_type=F32) + b1_ref[...]))
    hid = jnp.sin(fq * (jnp.dot(hid, w2_ref[...], precision=hp, preferred_element_type=F32) + b2_ref[...]))
    hid = jnp.sin(fq * (jnp.dot(hid, w3_ref[...], precision=hp, preferred_element_type=F32) + b3_ref[...]))
    h = jnp.dot(hid, fo_ref[...], precision=hp, preferred_element_type=F32)
    h = h * jnp.exp(-t[:, :1] * dl_ref[...])
    h_ref[...] = h.astype(h_ref.dtype)

    @pl.when(i == 0)
    def _():
        ss_ref[...] = jnp.zeros_like(ss_ref)

    ss_ref[...] += jnp.sum(h * h, axis=0, keepdims=True)


def _pad2(a, rows, cols):
    return jnp.pad(a.astype(F32), ((0, rows - a.shape[0]), (0, cols - a.shape[1])))


def _hyena_filters(seq_len, w1, b1, w2, b2, w3, b3, freq, f_out, tl=512):
    ffn = w1.shape[1]
    width2 = f_out.shape[1]
    bands = (HY_EMB - 1) // 2
    fr = np.linspace(1e-4, bands - 1, bands, dtype=np.float32)
    frq = np.zeros((1, LANES), np.float32)
    frq[0, 1:1 + bands] = fr
    frq[0, 1 + bands:1 + 2 * bands] = fr
    deltas = np.abs(np.linspace(math.log(HY_TARGET) / HY_SLOW_PCT, math.log(HY_TARGET) / HY_FAST_PCT,
                                width2 // 2, dtype=np.float32))
    dl = np.tile(deltas, 2)[None, :]
    row = lambda a: _pad2(a.reshape(1, -1), 1, LANES)
    small = lambda shape: _resident(shape, lambda i: (0, 0))
    return pl.pallas_call(
        functools.partial(_hy_filter_kernel, seq_len=seq_len),
        grid=(seq_len // tl,),
        in_specs=[small((1, LANES)), small((LANES, LANES)), small((1, LANES)), small((LANES, LANES)),
                  small((1, LANES)), small((LANES, LANES)), small((1, LANES)), small((1, LANES)),
                  small((LANES, width2)), small((1, width2))],
        out_specs=[pl.BlockSpec((tl, width2), lambda i: (i, 0)),
                   pl.BlockSpec((1, width2), lambda i: (0, 0))],
        out_shape=[jax.ShapeDtypeStruct((seq_len, width2), BF16),
                   jax.ShapeDtypeStruct((1, width2), F32)],
        compiler_params=_params("arbitrary"),
        name="hyena_filters",
    )(jnp.asarray(frq), _pad2(w1, LANES, LANES), row(b1), _pad2(w2, LANES, LANES), row(b2),
      _pad2(w3, LANES, LANES), row(b3), row(freq), _pad2(f_out, LANES, width2), jnp.asarray(dl))


HY_N1 = 64
HY_K1_GROUP = 8


def _hy_tables(seq_len):
    n = 2 * seq_len
    n1 = HY_N1
    n2 = n // n1
    k1 = np.arange(n1, dtype=np.float64)[:, None]
    t1 = np.arange(n1 // 2, dtype=np.float64)[None, :]
    psi = 2.0 * np.pi * (2 * k1 + 1) * t1 / (2 * n1)
    f1 = np.concatenate([np.cos(psi), -np.sin(psi)], axis=0)
    g1r, g1i = np.cos(psi).T, -np.sin(psi).T
    k2 = np.arange(n2 // 2, dtype=np.float64)[:, None]
    t2 = np.arange(n2, dtype=np.float64)[None, :]
    al = 2.0 * np.pi * k2 * t2 / n2
    f2 = np.block([[np.cos(al), np.sin(al)], [-np.sin(al), np.cos(al)]])
    g2 = np.block([[np.cos(al).T, -np.sin(al).T], [np.sin(al).T, np.cos(al).T]])
    th = 2.0 * np.pi * (2 * k1 + 1) * t2 / (2 * n)
    bf = lambda a: jnp.asarray(a, F32).astype(BF16)
    lanes = lambda a: jnp.broadcast_to(jnp.asarray(a, F32)[:, :, None], (n1, n2, LANES))
    return dict(f1=bf(f1), g1r=bf(g1r), g1i=bf(g1i), f2=bf(f2), g2=bf(g2),
                twc=lanes(np.cos(th)), tws=lanes(np.sin(th)), n1=n1, n2=n2)


def _hy_pre_kernel(x0_ref, x1_ref, v_ref, w_ref, b_ref, u_ref, x0c_ref):
    seq_len = x0_ref.shape[1]
    row = lax.broadcasted_iota(jnp.int32, (seq_len, LANES), 0)

    def conv(z_ref, i):
        z = z_ref[0].astype(F32)
        zm = jnp.where(row == 0, 0.0, pltpu.roll(z, 1, 0))
        zp = jnp.where(row == seq_len - 1, 0.0, pltpu.roll(z, seq_len - 1, 0))
        w = w_ref[i]
        return w[0:1] * zm + w[1:2] * z + w[2:3] * zp + b_ref[i]

    x0c_ref[0] = conv(x0_ref, 0).astype(x0c_ref.dtype)
    u_ref[0] = (conv(v_ref, 2) * conv(x1_ref, 1)).astype(u_ref.dtype)


def _hyena_pre(p3, lane0, conv_w, conv_b):
    bsz, seq_len, _ = p3.shape
    width = conv_w.shape[1] // 3
    nt = width // LANES
    off = lane0 // LANES
    w3 = conv_w.astype(F32).reshape(conv_w.shape[0], 3, width).transpose(1, 0, 2)
    b3 = conv_b.astype(F32).reshape(3, 1, width)
    blk = (1, seq_len, LANES)
    out = jax.ShapeDtypeStruct((bsz, seq_len, width), BF16)
    return pl.pallas_call(
        _hy_pre_kernel,
        grid=(bsz, nt),
        in_specs=[pl.BlockSpec(blk, lambda b, j: (b, 0, off + j)),
                  pl.BlockSpec(blk, lambda b, j: (b, 0, off + nt + j)),
                  pl.BlockSpec(blk, lambda b, j: (b, 0, off + 2 * nt + j)),
                  pl.BlockSpec((3, conv_w.shape[0], LANES), lambda b, j: (0, 0, j)),
                  pl.BlockSpec((3, 1, LANES), lambda b, j: (0, 0, j))],
        out_specs=[pl.BlockSpec(blk, lambda b, j: (b, 0, j)),
                   pl.BlockSpec(blk, lambda b, j: (b, 0, j))],
        out_shape=[out, out],
        compiler_params=_params("parallel", "parallel"),
        name="hyena_pre",
    )(p3, p3, p3, w3, b3)


def _hy_stage1_kernel(f_ref, x_ref, a_ref):
    a_ref[0] = jnp.dot(f_ref[...], x_ref[0], preferred_element_type=F32).astype(a_ref.dtype)


def _hy_stage1(x, f1, n2, tl=8192):
    bsz, seq_len, width = x.shape
    m, k = f1.shape
    assert seq_len == k * n2
    a = pl.pallas_call(
        _hy_stage1_kernel,
        grid=(bsz, n2 * width // tl),
        in_specs=[_resident((m, k), lambda b, j: (0, 0)),
                  pl.BlockSpec((1, k, tl), lambda b, j: (b, 0, j))],
        out_specs=pl.BlockSpec((1, m, tl), lambda b, j: (b, 0, j)),
        out_shape=jax.ShapeDtypeStruct((bsz, m, n2 * width), BF16),
        compiler_params=_params("parallel", "parallel"),
        name="hyena_stage1",
    )(f1, x.reshape(bsz, k, n2 * width))
    return a.reshape(bsz, m, n2, width)


def _lane_tile(a, width):
    return jnp.concatenate([a] * (width // a.shape[1]), axis=1)


def _hy_mid_forward(ar, ai, c, s, f2_ref):
    bst = jnp.concatenate([ar * c + ai * s, ai * c - ar * s], axis=0).astype(BF16)
    return jnp.dot(f2_ref[...], bst, preferred_element_type=F32)


def _hy_filter_mid_kernel(are_ref, aim_ref, twc_ref, tws_ref, f2_ref, ss_ref, h_ref):
    width = h_ref.shape[2]
    half = f2_ref.shape[0] // 2
    ss = ss_ref[...]
    scale = lax.rsqrt(ss[:, :width] + ss[:, width:])
    for g in range(h_ref.shape[0]):
        c = _lane_tile(twc_ref[g], 2 * width)
        s = _lane_tile(tws_ref[g], 2 * width)
        x = _hy_mid_forward(are_ref[0, g].astype(F32), aim_ref[0, g].astype(F32), c, s, f2_ref)
        h_ref[g, :half, :] = (x[:half, :width] + x[:half, width:]) * scale
        h_ref[g, half:, :] = (x[half:, :width] - x[half:, width:]) * scale


def _hy_mid_kernel(are_ref, aim_ref, twc_ref, tws_ref, f2_ref, g2_ref, h_ref, dre_ref, dim_ref):
    width = h_ref.shape[2]
    n2 = f2_ref.shape[0]
    half = n2 // 2
    for g in range(h_ref.shape[0]):
        c = _lane_tile(twc_ref[g], width)
        s = _lane_tile(tws_ref[g], width)
        x = _hy_mid_forward(are_ref[0, g].astype(F32), aim_ref[0, g].astype(F32), c, s, f2_ref)
        xr, xi = x[:half], x[half:]
        hr, hi = h_ref[g, :half, :], h_ref[g, half:, :]
        yst = jnp.concatenate([xr * hr - xi * hi, xr * hi + xi * hr], axis=0).astype(BF16)
        cc = jnp.dot(g2_ref[...], yst, preferred_element_type=F32)
        cr, ci = cc[:n2], cc[n2:]
        dre_ref[0, g] = (cr * c - ci * s).astype(dre_ref.dtype)
        dim_ref[0, g] = (cr * s + ci * c).astype(dim_ref.dtype)


def _hy_stage3_kernel(gr_ref, gi_ref, dre_ref, dim_ref, u_ref, x0_ref, skip_ref, o_ref, *, scale):
    y = (jnp.dot(gr_ref[...], dre_ref[0], preferred_element_type=F32)
         + jnp.dot(gi_ref[...], dim_ref[0], preferred_element_type=F32)) * np.float32(scale)
    y = y + u_ref[0].astype(F32) * skip_ref[...]
    o_ref[0] = (y * x0_ref[0].astype(F32)).astype(o_ref.dtype)


def _hyena_long_conv(u, x0c, h, ss, skip, tl=8192):
    bsz, seq_len, width = u.shape
    tb = _hy_tables(seq_len)
    n1, n2, grp = tb["n1"], tb["n2"], HY_K1_GROUP
    ng = n1 // grp
    tw_spec = lambda im: pl.BlockSpec((grp, n2, LANES), im)
    ah = _hy_stage1(h[None], tb["f1"], n2)
    hblk = (1, grp, n2, 2 * width)
    hhat = pl.pallas_call(
        _hy_filter_mid_kernel,
        grid=(ng,),
        in_specs=[pl.BlockSpec(hblk, lambda i: (0, i, 0, 0)),
                  pl.BlockSpec(hblk, lambda i: (0, ng + i, 0, 0)),
                  tw_spec(lambda i: (i, 0, 0)), tw_spec(lambda i: (i, 0, 0)),
                  _resident((n2, 2 * n2), lambda i: (0, 0)),
                  _resident((1, 2 * width), lambda i: (0, 0))],
        out_specs=pl.BlockSpec((grp, n2, width), lambda i: (i, 0, 0)),
        out_shape=jax.ShapeDtypeStruct((n1, n2, width), F32),
        compiler_params=_params("parallel"),
        name="hyena_filter_mid",
    )(ah, ah, tb["twc"], tb["tws"], tb["f2"], ss)
    a = _hy_stage1(u, tb["f1"], n2)
    blk = (1, grp, n2, width)
    dshape = jax.ShapeDtypeStruct((bsz, n1, n2, width), BF16)
    dre, dim = pl.pallas_call(
        _hy_mid_kernel,
        grid=(ng, bsz),
        in_specs=[pl.BlockSpec(blk, lambda i, b: (b, i, 0, 0)),
                  pl.BlockSpec(blk, lambda i, b: (b, ng + i, 0, 0)),
                  tw_spec(lambda i, b: (i, 0, 0)), tw_spec(lambda i, b: (i, 0, 0)),
                  _resident((n2, 2 * n2), lambda i, b: (0, 0)),
                  _resident((2 * n2, n2), lambda i, b: (0, 0)),
                  pl.BlockSpec((grp, n2, width), lambda i, b: (i, 0, 0))],
        out_specs=[pl.BlockSpec(blk, lambda i, b: (b, i, 0, 0)),
                   pl.BlockSpec(blk, lambda i, b: (b, i, 0, 0))],
        out_shape=[dshape, dshape],
        compiler_params=_params("parallel", "parallel"),
        name="hyena_mid",
    )(a, a, tb["twc"], tb["tws"], tb["f2"], tb["g2"], hhat)
    t1n = n1 // 2
    merged = n2 * width
    tile = lambda rows: pl.BlockSpec((1, rows, tl), lambda b, j: (b, 0, j))
    y = pl.pallas_call(
        functools.partial(_hy_stage3_kernel, scale=1.0 / seq_len),
        grid=(bsz, merged // tl),
        in_specs=[_resident((t1n, n1), lambda b, j: (0, 0)),
                  _resident((t1n, n1), lambda b, j: (0, 0)),
                  tile(n1), tile(n1), tile(t1n), tile(t1n),
                  _resident((1, tl), lambda b, j: (0, 0))],
        out_specs=tile(t1n),
        out_shape=jax.ShapeDtypeStruct((bsz, t1n, merged), BF16),
        compiler_params=_params("parallel", "parallel"),
        name="hyena_stage3",
    )(tb["g1r"], tb["g1i"], dre.reshape(bsz, n1, merged), dim.reshape(bsz, n1, merged),
      u.reshape(bsz, t1n, merged), x0c.reshape(bsz, t1n, merged),
      jnp.tile(skip.astype(F32), tl // width).reshape(1, tl))
    return y.reshape(bsz, seq_len, width)


def _sgu_kernel(zu_ref, zv_ref, g_ref, b_ref, w_ref, sb_ref, o_ref):
    tokens, width = zu_ref.shape[1], zu_ref.shape[2]
    u = _gelu(zu_ref[0].astype(F32))
    v = _layer_norm(_gelu(zv_ref[0].astype(F32)), g_ref[...], b_ref[...]).astype(BF16)
    lane = lax.broadcasted_iota(jnp.int32, (SG_CHUNK, LANES), 1)
    first = lane < LANES // 2
    for c in range(tokens // SG_CHUNK):
        rows = slice(c * SG_CHUNK, (c + 1) * SG_CHUNK)
        for p in range(width // LANES):
            cols = slice(p * LANES, (p + 1) * LANES)
            r = jnp.dot(w_ref[p], v[rows, cols], preferred_element_type=F32)
            sv = jnp.where(first, r[:SG_CHUNK], r[SG_CHUNK:]) + sb_ref[:, cols]
            o_ref[0, rows, cols] = (u[rows, cols] * sv).astype(o_ref.dtype)


def _spatial_gating(p3, ln_g, ln_b, sg_w, sg_b, tokens=512):
    bsz, seq_len, _ = p3.shape
    groups = sg_w.shape[0]
    width = ln_g.shape[0]
    gw = width // groups
    assert 2 * gw == LANES and sg_w.shape[1] == SG_CHUNK
    wst = sg_w.astype(BF16).reshape(groups // 2, 2 * SG_CHUNK, SG_CHUNK)
    sb = jnp.repeat(sg_b.astype(F32).T, gw, axis=1)
    blk = (1, tokens, width)
    return pl.pallas_call(
        _sgu_kernel,
        grid=(bsz, seq_len // tokens),
        in_specs=[pl.BlockSpec(blk, lambda b, i: (b, i, 0)),
                  pl.BlockSpec(blk, lambda b, i: (b, i, 1)),
                  _resident((1, width), lambda b, i: (0, 0)),
                  _resident((1, width), lambda b, i: (0, 0)),
                  _resident((groups // 2, 2 * SG_CHUNK, SG_CHUNK), lambda b, i: (0, 0, 0)),
                  _resident((SG_CHUNK, width), lambda b, i: (0, 0))],
        out_specs=pl.BlockSpec(blk, lambda b, i: (b, i, 0)),
        out_shape=jax.ShapeDtypeStruct((bsz, seq_len, width), BF16),
        compiler_params=_params("parallel", "parallel"),
        name="spatial_gating",
    )(p3, p3, ln_g.astype(F32).reshape(1, width), ln_b.astype(F32).reshape(1, width), wst, sb)


CV_PAD = 16
CV_ROWS = 64


def _conf_kernel(za_ref, zg_ref, w_ref, wb_ref, g_ref, b_ref, o_ref, apad_ref):
    seq_len, width = za_ref.shape[1], za_ref.shape[2]
    fill_rows = 256
    pad_tiles = CV_PAD // SUBLANES
    out_tiles = CV_ROWS // SUBLANES
    apad_ref[0:pad_tiles] = jnp.zeros((pad_tiles, SUBLANES, width), F32)
    apad_ref[seq_len // SUBLANES + pad_tiles:seq_len // SUBLANES + 2 * pad_tiles] = jnp.zeros(
        (pad_tiles, SUBLANES, width), F32)

    def fill(i, carry):
        r0 = pl.multiple_of(i * fill_rows, fill_rows)
        a = za_ref[0, pl.ds(r0, fill_rows), :].astype(F32)
        g = zg_ref[0, pl.ds(r0, fill_rows), :].astype(F32)
        apad_ref[pl.ds(i * (fill_rows // SUBLANES) + pad_tiles, fill_rows // SUBLANES)] = (
            (a * jax.nn.sigmoid(g)).reshape(fill_rows // SUBLANES, SUBLANES, width))
        return carry

    lax.fori_loop(0, seq_len // fill_rows, fill, 0)
    sub = lax.broadcasted_iota(jnp.int32, (out_tiles, SUBLANES, width), 1)

    def conv(i, carry):
        r0 = pl.multiple_of(i * CV_ROWS, CV_ROWS)
        off0 = CV_PAD - CV_KERNEL // 2
        acc = None
        for s in range(SUBLANES):
            part = None
            for q in range((off0 + CV_KERNEL - 1) // SUBLANES + 1):
                j = SUBLANES * q + s - off0
                if 0 <= j < CV_KERNEL:
                    term = w_ref[j][None] * apad_ref[pl.ds(i * out_tiles + q, out_tiles + 1)]
                    part = term if part is None else part + term
            if s:
                rot = pltpu.roll(part, SUBLANES - s, 1)
                part = jnp.where(sub < SUBLANES - s, rot[:out_tiles], rot[1:])
            else:
                part = part[:out_tiles]
            acc = part if acc is None else acc + part
        y = _layer_norm(acc + wb_ref[...][None], g_ref[...][None], b_ref[...][None])
        y = y * jax.nn.sigmoid(y)
        o_ref[0, pl.ds(r0, CV_ROWS), :] = y.reshape(CV_ROWS, width).astype(o_ref.dtype)
        return carry

    lax.fori_loop(0, seq_len // CV_ROWS, conv, 0)


def _conformer_conv(p3, lane_block, dw_w, dw_b, ln_g, ln_b):
    bsz, seq_len, _ = p3.shape
    width = dw_w.shape[1]
    blk = (1, seq_len, width)
    vec = lambda a: a.astype(F32).reshape(1, width)
    w8 = jnp.broadcast_to(dw_w.astype(F32)[:, None, :], (CV_KERNEL, SUBLANES, width))
    return pl.pallas_call(
        _conf_kernel,
        grid=(bsz,),
        in_specs=[pl.BlockSpec(blk, lambda b: (b, 0, lane_block)),
                  pl.BlockSpec(blk, lambda b: (b, 0, lane_block + 1)),
                  _resident((CV_KERNEL, SUBLANES, width), lambda b: (0, 0, 0)),
                  _resident((1, width), lambda b: (0, 0)),
                  _resident((1, width), lambda b: (0, 0)),
                  _resident((1, width), lambda b: (0, 0))],
        out_specs=pl.BlockSpec(blk, lambda b: (b, 0, 0)),
        out_shape=jax.ShapeDtypeStruct((bsz, seq_len, width), BF16),
        scratch_shapes=[pltpu.VMEM(((seq_len + 2 * CV_PAD) // SUBLANES, SUBLANES, width), F32)],
        compiler_params=_params("parallel"),
        name="conformer_conv",
    )(p3, p3, w8, vec(dw_b), vec(ln_g), vec(ln_b))


def _out_mlp_kernel(x_ref, a_ref, b_ref, woa_ref, wob_ref, g_ref, w1_ref, w2_ref, gf_ref, o_ref,
                    *, hidden_chunk, final):
    x1 = (x_ref[...]
          + jnp.dot(a_ref[...], woa_ref[...], preferred_element_type=F32)
          + jnp.dot(b_ref[...], wob_ref[...], preferred_element_type=F32))
    hn = _rms(x1, g_ref[...]).astype(BF16)
    mlp = None
    for c in range(w1_ref.shape[1] // hidden_chunk):
        cols = slice(c * hidden_chunk, (c + 1) * hidden_chunk)
        t = jnp.maximum(jnp.dot(hn, w1_ref[:, cols], preferred_element_type=F32), 0.0)
        y = jnp.dot((t * t).astype(BF16), w2_ref[cols, :], preferred_element_type=F32)
        mlp = y if mlp is None else mlp + y
    acc = x1 + mlp
    if final:
        acc = _rms(acc, gf_ref[...])
    o_ref[...] = acc


def _out_mlp(x2, a, b, w_out, g, w1, w2, gf, final, tm=512, hidden_chunk=1024):
    m, d = x2.shape
    half = a.shape[1]
    hidden = w1.shape[1]
    return pl.pallas_call(
        functools.partial(_out_mlp_kernel, hidden_chunk=hidden_chunk, final=final),
        grid=(m // tm,),
        in_specs=[pl.BlockSpec((tm, d), lambda i: (i, 0)),
                  pl.BlockSpec((tm, half), lambda i: (i, 0)),
                  pl.BlockSpec((tm, half), lambda i: (i, 0)),
                  _resident((half, d), lambda i: (0, 0)),
                  _resident((half, d), lambda i: (1, 0)),
                  _resident((1, d), lambda i: (0, 0)),
                  _resident((d, hidden), lambda i: (0, 0)),
                  _resident((hidden, d), lambda i: (0, 0)),
                  _resident((1, d), lambda i: (0, 0))],
        out_specs=pl.BlockSpec((tm, d), lambda i: (i, 0)),
        out_shape=jax.ShapeDtypeStruct((m, d), F32),
        compiler_params=_params("parallel"),
        name="out_mlp",
    )(x2, a, b, w_out, w_out, g.reshape(1, d), w1, w2, gf.reshape(1, d))


def kernel(x, norm_g, ev_w_in, ev_rpb, hy_conv_w, hy_conv_b, hy_w1, hy_b1, hy_w2, hy_b2, hy_w3, hy_b3, hy_freq, hy_w_out, hy_skip, ev_w_out, od_w_in, sg_ln_g, sg_ln_b, sg_w, sg_b, cv_dw_w, cv_dw_b, cv_ln_g, cv_ln_b, od_w_out, mlp_w1, mlp_w2, final_g):
    bsz, seq_len, d = x.shape
    depth = norm_g.shape[0]
    m = bsz * seq_len
    x2 = x.astype(F32).reshape(m, d)
    na_width = ev_rpb.shape[1] * NA_HEAD_DIM
    for i in range(depth):
        j = i // 2
        if i % 2 == 0:
            p3 = _norm_proj(x2, norm_g[i, 0], ev_w_in[j].astype(BF16)).reshape(bsz, seq_len, -1)
            mix_a = _neighbourhood_attention(p3, ev_rpb[j])
            u, x0c = _hyena_pre(p3, 3 * na_width, hy_conv_w[j], hy_conv_b[j])
            h, ss = _hyena_filters(seq_len, hy_w1[j], hy_b1[j], hy_w2[j], hy_b2[j], hy_w3[j], hy_b3[j],
                                   hy_freq[j], hy_w_out[j])
            mix_b = _hyena_long_conv(u, x0c, h, ss, hy_skip[j])
            w_out = ev_w_out[j]
        else:
            p3 = _norm_proj(x2, norm_g[i, 0], od_w_in[j].astype(BF16)).reshape(bsz, seq_len, -1)
            mix_a = _spatial_gating(p3, sg_ln_g[j], sg_ln_b[j], sg_w[j], sg_b[j])
            mix_b = _conformer_conv(p3, 2, cv_dw_w[j], cv_dw_b[j], cv_ln_g[j], cv_ln_b[j])
            w_out = od_w_out[j]
        x2 = _out_mlp(x2, mix_a.reshape(m, -1), mix_b.reshape(m, -1), w_out.astype(BF16), norm_g[i, 1],
                      mlp_w1[i].astype(BF16), mlp_w2[i].astype(BF16), final_g, final=(i == depth - 1))
    return x2.reshape(bsz, seq_len, d).astype(x.dtype)
```

```python
import functools
import math

import numpy as np
import jax
import jax.numpy as jnp
from jax import lax
from jax.experimental import pallas as pl
from jax.experimental.pallas import tpu as pltpu

F32 = jnp.float32
BF16 = jnp.bfloat16
EPS = 1e-6
MASK_VALUE = -1e30

GRID_W = 64
NA_HEAD_DIM = 64
NA_WIN_R = 8
NA_WIN_C = 16
HY_EMB = 33
HY_TARGET = 1e-2
HY_FAST_PCT = 0.3
HY_SLOW_PCT = 1.5
SG_CHUNK = 128
CV_KERNEL = 31

LANES = 128
SUBLANES = 8
VMEM_LIMIT = 56 * 1024 * 1024


def _params(*sem):
    return pltpu.CompilerParams(dimension_semantics=sem, vmem_limit_bytes=VMEM_LIMIT)


def _resident(shape, index_map):
    return pl.BlockSpec(shape, index_map, pipeline_mode=pl.Buffered(1))


def _rms(x, g):
    return x * lax.rsqrt(jnp.mean(x * x, axis=-1, keepdims=True) + EPS) * g


def _layer_norm(x, g, b):
    xc = x - jnp.mean(x, axis=-1, keepdims=True)
    return xc * lax.rsqrt(jnp.mean(xc * xc, axis=-1, keepdims=True) + EPS) * g + b


def _gelu(x):
    return x * (lax.erf(x / np.float32(math.sqrt(2.0))) + 1.0) / 2.0


def _norm_proj_kernel(x_ref, g_ref, w_ref, o_ref):
    hn = _rms(x_ref[...], g_ref[...]).astype(BF16)
    o_ref[...] = jnp.dot(hn, w_ref[...], preferred_element_type=F32).astype(o_ref.dtype)


def _norm_proj(x2, g, w, tm=512):
    m, d = x2.shape
    n = w.shape[1]
    return pl.pallas_call(
        _norm_proj_kernel,
        grid=(m // tm,),
        in_specs=[pl.BlockSpec((tm, d), lambda i: (i, 0)),
                  _resident((1, d), lambda i: (0, 0)),
                  _resident((d, n), lambda i: (0, 0))],
        out_specs=pl.BlockSpec((tm, n), lambda i: (i, 0)),
        out_shape=jax.ShapeDtypeStruct((m, n), BF16),
        compiler_params=_params("parallel"),
        name="norm_proj",
    )(x2, g.reshape(1, d), w)


NA_UNROLL = 8


def _na_kernel(q_ref, k_ref, v_ref, tb_ref, o_ref):
    seq_len = q_ref.shape[1]
    rows = seq_len // GRID_W
    nkeys = NA_WIN_R * GRID_W
    lane = lax.broadcasted_iota(jnp.int32, (GRID_W, LANES), 1)
    first = lane < NA_HEAD_DIM

    def body(i, carry):
        q0s, k0s, cases, scores = [], [], [], []
        for u in range(NA_UNROLL):
            r = i * NA_UNROLL + u
            rs = jnp.clip(r - NA_WIN_R // 2, 0, rows - NA_WIN_R)
            cases.append(r - rs)
            q0s.append(pl.multiple_of(r * GRID_W, GRID_W))
            k0s.append(pl.multiple_of(rs * GRID_W, GRID_W))
            q2 = q_ref[0, pl.ds(q0s[u], GRID_W), :] * BF16(NA_HEAD_DIM ** -0.5)
            k2 = k_ref[0, pl.ds(k0s[u], nkeys), :]
            zero = jnp.zeros_like(q2)
            qs = jnp.concatenate([jnp.where(first, q2, zero), jnp.where(first, zero, q2)], axis=0)
            scores.append(lax.dot_general(qs, k2, (((1,), (1,)), ((), ())), preferred_element_type=F32))
        probs, dens = [], []
        for u in range(NA_UNROLL):
            s = scores[u] + tb_ref[0, cases[u]]
            e = jnp.exp(s - jnp.max(s, axis=-1, keepdims=True))
            dens.append(jnp.sum(e, axis=-1, keepdims=True))
            probs.append(e.astype(BF16))
        outs = [jnp.dot(probs[u], v_ref[0, pl.ds(k0s[u], nkeys), :], preferred_element_type=F32)
                for u in range(NA_UNROLL)]
        for u in range(NA_UNROLL):
            o = outs[u] / dens[u]
            o_ref[0, pl.ds(q0s[u], GRID_W), :] = jnp.where(first, o[:GRID_W], o[GRID_W:]).astype(o_ref.dtype)
        return carry

    lax.fori_loop(0, rows // NA_UNROLL, body, 0)


def _na_bias_table(rpb):
    heads, nrow, ncol = rpb.shape
    qc = np.arange(GRID_W)[:, None]
    kc = np.arange(GRID_W)[None, :]
    cs = np.clip(qc - NA_WIN_C // 2, 0, GRID_W - NA_WIN_C)
    inside = (kc >= cs) & (kc < cs + NA_WIN_C)
    col_off = kc - qc + (NA_WIN_C - 1)
    onehot = (col_off[None] == np.arange(ncol)[:, None, None]) & inside[None]
    expand = jnp.asarray(onehot.reshape(ncol, GRID_W * GRID_W), F32)
    t = jnp.dot(rpb.astype(F32).reshape(heads * nrow, ncol), expand, precision=lax.Precision.HIGHEST)
    t = jnp.where(inside[None, None], t.reshape(heads, nrow, GRID_W, GRID_W), MASK_VALUE)
    t = jnp.stack([t[:, NA_WIN_R - 1 - c:2 * NA_WIN_R - 1 - c] for c in range(NA_WIN_R)], axis=1)
    t = t.transpose(0, 1, 3, 2, 4).reshape(heads // 2, 2, NA_WIN_R, GRID_W, NA_WIN_R * GRID_W)
    return t.transpose(0, 2, 1, 3, 4).reshape(heads // 2, NA_WIN_R, 2 * GRID_W, NA_WIN_R * GRID_W)


def _neighbourhood_attention(p3, rpb):
    bsz, seq_len, _ = p3.shape
    heads = rpb.shape[0]
    pairs = heads // 2
    tb = _na_bias_table(rpb)
    blk = (1, seq_len, LANES)
    return pl.pallas_call(
        _na_kernel,
        grid=(bsz, pairs),
        in_specs=[pl.BlockSpec(blk, lambda b, j: (b, 0, j)),
                  pl.BlockSpec(blk, lambda b, j: (b, 0, pairs + j)),
                  pl.BlockSpec(blk, lambda b, j: (b, 0, 2 * pairs + j)),
                  pl.BlockSpec((1, NA_WIN_R, 2 * GRID_W, NA_WIN_R * GRID_W), lambda b, j: (j, 0, 0, 0))],
        out_specs=pl.BlockSpec(blk, lambda b, j: (b, 0, j)),
        out_shape=jax.ShapeDtypeStruct((bsz, seq_len, pairs * LANES), BF16),
        compiler_params=_params("parallel", "parallel"),
        name="na_attention",
    )(p3, p3, p3, tb)


def _hy_filter_kernel(frq_ref, w1_ref, b1_ref, w2_ref, b2_ref, w3_ref, b3_ref, fq_ref, fo_ref,
                      dl_ref, h_ref, ss_ref, *, seq_len):
    i = pl.program_id(0)
    tl = h_ref.shape[0]
    hp = lax.Precision.HIGHEST
    pos = (lax.broadcasted_iota(jnp.int32, (tl, LANES), 0) + i * tl).astype(F32)
    lane = lax.broadcasted_iota(jnp.int32, (tl, LANES), 1)
    bands = (HY_EMB - 1) // 2
    t = pos / np.float32(seq_len - 1)
    ang = (np.float32(2.0 * math.pi) * pos / np.float32(seq_len)) * frq_ref[...]
    z = jnp.where(lane == 0, t,
                  jnp.where(lane <= bands, jnp.cos(ang),
                            jnp.where(lane <= 2 * bands, -jnp.sin(ang), 0.0)))
    fq = fq_ref[...]
    hid = jnp.sin(fq * (jnp.dot(z, w1_ref[...], precision=hp, preferred_element_type=F32) + b1_ref[...]))
    hid = jnp.sin(fq * (jnp.dot(hid, w2_ref[...], precision=hp, preferred_element_type=F32) + b2_ref[...]))
    hid = jnp.sin(fq * (jnp.dot(hid, w3_ref[...], precision=hp, preferred_element_type=F32) + b3_ref[...]))
    h = jnp.dot(hid, fo_ref[...], precision=hp, preferred_element_type=F32)
    h = h * jnp.exp(-t[:, :1] * dl_ref[...])
    h_ref[...] = h.astype(h_ref.dtype)

    @pl.when(i == 0)
    def _():
        ss_ref[...] = jnp.zeros_like(ss_ref)

    ss_ref[...] += jnp.sum(h * h, axis=0, keepdims=True)


def _pad2(a, rows, cols):
    return jnp.pad(a.astype(F32), ((0, rows - a.shape[0]), (0, cols - a.shape[1])))


def _hyena_filters(seq_len, w1, b1, w2, b2, w3, b3, freq, f_out, tl=512):
    ffn = w1.shape[1]
    width2 = f_out.shape[1]
    bands = (HY_EMB - 1) // 2
    fr = np.linspace(1e-4, bands - 1, bands, dtype=np.float32)
    frq = np.zeros((1, LANES), np.float32)
    frq[0, 1:1 + bands] = fr
    frq[0, 1 + bands:1 + 2 * bands] = fr
    deltas = np.abs(np.linspace(math.log(HY_TARGET) / HY_SLOW_PCT, math.log(HY_TARGET) / HY_FAST_PCT,
                                width2 // 2, dtype=np.float32))
    dl = np.tile(deltas, 2)[None, :]
    row = lambda a: _pad2(a.reshape(1, -1), 1, LANES)
    small = lambda shape: _resident(shape, lambda i: (0, 0))
    return pl.pallas_call(
        functools.partial(_hy_filter_kernel, seq_len=seq_len),
        grid=(seq_len // tl,),
        in_specs=[small((1, LANES)), small((LANES, LANES)), small((1, LANES)), small((LANES, LANES)),
                  small((1, LANES)), small((LANES, LANES)), small((1, LANES)), small((1, LANES)),
                  small((LANES, width2)), small((1, width2))],
        out_specs=[pl.BlockSpec((tl, width2), lambda i: (i, 0)),
                   pl.BlockSpec((1, width2), lambda i: (0, 0))],
        out_shape=[jax.ShapeDtypeStruct((seq_len, width2), BF16),
                   jax.ShapeDtypeStruct((1, width2), F32)],
        compiler_params=_params("arbitrary"),
        name="hyena_filters",
    )(jnp.asarray(frq), _pad2(w1, LANES, LANES), row(b1), _pad2(w2, LANES, LANES), row(b2),
      _pad2(w3, LANES, LANES), row(b3), row(freq), _pad2(f_out, LANES, width2), jnp.asarray(dl))


HY_TK = 256


def _hy_tables(seq_len):
    n = 2 * seq_len
    nt, nk = seq_len // 4, n // 8
    kap = jnp.arange(nk, dtype=jnp.int32)[:, None]
    tau = jnp.arange(nt, dtype=jnp.int32)[None, :]
    phase = ((2 * kap + 1) * tau) % (n // 2)
    ang = phase.astype(F32) * np.float32(2.0 * math.pi / (n // 2))
    c = jnp.cos(ang).reshape(nk // HY_TK, 1, HY_TK, nt)
    s = (-jnp.sin(ang)).reshape(nk // HY_TK, 1, HY_TK, nt)
    f4 = jnp.concatenate([c, s], axis=1).reshape(2 * nk, nt).astype(BF16)
    rho = jnp.arange(1, 4, dtype=jnp.int32)[:, None]
    th = ((2 * kap.reshape(1, nk) + 1) * rho).astype(F32) * np.float32(2.0 * math.pi / (2 * n))
    lanes = lambda a: jnp.broadcast_to(a[:, :, None], (3, nk, LANES))
    return dict(f4=f4, g4=f4.T, twc=lanes(jnp.cos(th)), tws=lanes(jnp.sin(th)), nt=nt, nk=nk)


HY_RADIX = 4


def _hy_pre_kernel(x0_ref, x1_ref, v_ref, w_ref, b_ref, u_ref, x0c_ref, tmp_ref):
    seq_len = x0_ref.shape[1]
    row = lax.broadcasted_iota(jnp.int32, (seq_len, LANES), 0)

    def conv(z_ref, i):
        z = z_ref[0].astype(F32)
        zm = jnp.where(row == 0, 0.0, pltpu.roll(z, 1, 0))
        zp = jnp.where(row == seq_len - 1, 0.0, pltpu.roll(z, seq_len - 1, 0))
        w = w_ref[i]
        return w[0:1] * zm + w[1:2] * z + w[2:3] * zp + b_ref[i]

    def store_merged(o_ref, val):
        tmp_ref[...] = val
        for r in range(HY_RADIX):
            o_ref[0, :, r * LANES:(r + 1) * LANES] = tmp_ref[pl.ds(r, seq_len // HY_RADIX, stride=HY_RADIX), :].astype(o_ref.dtype)

    store_merged(x0c_ref, conv(x0_ref, 0))
    store_merged(u_ref, conv(v_ref, 2) * conv(x1_ref, 1))


def _hyena_pre(p3, lane0, conv_w, conv_b):
    bsz, seq_len, _ = p3.shape
    width = conv_w.shape[1] // 3
    nt = width // LANES
    off = lane0 // LANES
    w3 = conv_w.astype(F32).reshape(conv_w.shape[0], 3, width).transpose(1, 0, 2)
    b3 = conv_b.astype(F32).reshape(3, 1, width)
    blk = (1, seq_len, LANES)
    oblk = (1, seq_len // HY_RADIX, HY_RADIX * LANES)
    out = jax.ShapeDtypeStruct((bsz, seq_len // HY_RADIX, HY_RADIX * width), BF16)
    return pl.pallas_call(
        _hy_pre_kernel,
        grid=(bsz, nt),
        in_specs=[pl.BlockSpec(blk, lambda b, j: (b, 0, off + j)),
                  pl.BlockSpec(blk, lambda b, j: (b, 0, off + nt + j)),
                  pl.BlockSpec(blk, lambda b, j: (b, 0, off + 2 * nt + j)),
                  pl.BlockSpec((3, conv_w.shape[0], LANES), lambda b, j: (0, 0, j)),
                  pl.BlockSpec((3, 1, LANES), lambda b, j: (0, 0, j))],
        out_specs=[pl.BlockSpec(oblk, lambda b, j: (b, 0, j)),
                   pl.BlockSpec(oblk, lambda b, j: (b, 0, j))],
        out_shape=[out, out],
        scratch_shapes=[pltpu.VMEM((seq_len, LANES), F32)],
        compiler_params=_params("parallel", "parallel"),
        name="hyena_pre",
    )(p3, p3, p3, w3, b3)


def _lane_tile(a, width):
    return jnp.concatenate([a] * (width // a.shape[1]), axis=1)


def _rho_slice(a, r, width, chunk):
    return jnp.concatenate([a[:, (g * HY_RADIX + r) * chunk:(g * HY_RADIX + r + 1) * chunk]
                            for g in range(width // chunk)], axis=1)


def _hy_bins(q, twc_ref, tws_ref, width, chunk):
    tk = q.shape[0] // 2
    qr, qi = q[:tk], q[tk:]
    zr, zi = [_rho_slice(qr, 0, width, chunk)], [_rho_slice(qi, 0, width, chunk)]
    for r in range(1, HY_RADIX):
        c = _lane_tile(twc_ref[r - 1], width)
        s = _lane_tile(tws_ref[r - 1], width)
        a, b = _rho_slice(qr, r, width, chunk), _rho_slice(qi, r, width, chunk)
        zr.append(a * c + b * s)
        zi.append(b * c - a * s)
    t0r, t0i = zr[0] + zr[2], zi[0] + zi[2]
    t1r, t1i = zr[0] - zr[2], zi[0] - zi[2]
    t2r, t2i = zr[1] + zr[3], zi[1] + zi[3]
    t3r, t3i = zr[1] - zr[3], zi[1] - zi[3]
    return ((t0r + t2r, t0i + t2i), (t1r + t3i, t1i - t3r), (t1r - t3i, -t1i - t3r), (t0r - t2r, t2i - t0i))


def _hy_filter_kernel2(f_ref, h_ref, twc_ref, tws_ref, ss_ref, o_ref):
    width = o_ref.shape[2]
    q = jnp.dot(f_ref[...], h_ref[...], preferred_element_type=F32)
    ss = ss_ref[...]
    scale = lax.rsqrt(ss[:, :width] + ss[:, width:])
    for t, (xr, xi) in enumerate(_hy_bins(q, twc_ref, tws_ref, 2 * width, 2 * width)):
        o_ref[2 * t] = (xr[:, :width] + xr[:, width:]) * scale
        o_ref[2 * t + 1] = (xi[:, :width] - xi[:, width:]) * scale


def _hy_conv_kernel(f_ref, g_ref, u_ref, h_ref, twc_ref, tws_ref, x0_ref, skip_ref, o_ref, acc_ref, tmp_ref, *, scale):
    kt = pl.program_id(1)
    width = h_ref.shape[2]
    q = jnp.dot(f_ref[...], u_ref[0], preferred_element_type=F32)
    prod = []
    for t, (xr, xi) in enumerate(_hy_bins(q, twc_ref, tws_ref, width, LANES)):
        hr, hi = h_ref[2 * t], h_ref[2 * t + 1]
        prod.append((xr * hr - xi * hi, xr * hi + xi * hr))
    (yar, yai), (ybr, ybi), (ycr, yci), (ydr, ydi) = prod
    e0r, e0i = yar + ydr, yai - ydi
    e1r, e1i = yar - ydr, yai + ydi
    e2r, e2i = ybr + ycr, ybi - yci
    e3r, e3i = ybr - ycr, ybi + yci
    sr = [e0r + e2r, e1r - e3i, e0r - e2r, e1r + e3i]
    si = [e0i + e2i, e1i + e3r, e0i - e2i, e1i - e3r]
    vr, vi = [sr[0]], [si[0]]
    for r in range(1, HY_RADIX):
        c = _lane_tile(twc_ref[r - 1], width)
        s = _lane_tile(tws_ref[r - 1], width)
        vr.append(sr[r] * c - si[r] * s)
        vi.append(sr[r] * s + si[r] * c)
    merged = lambda parts: jnp.concatenate([parts[r][:, g * LANES:(g + 1) * LANES]
                                            for g in range(width // LANES) for r in range(HY_RADIX)], axis=1)
    v = jnp.concatenate([merged(vr), merged(vi)], axis=0).astype(BF16)
    part = jnp.dot(g_ref[...], v, preferred_element_type=F32)

    @pl.when(kt == 0)
    def _():
        acc_ref[...] = part

    @pl.when(kt > 0)
    def _():
        acc_ref[...] += part

    @pl.when(kt == pl.num_programs(1) - 1)
    def _():
        rows = acc_ref.shape[0]
        for g in range(width // LANES):
            for r in range(HY_RADIX):
                lanes = slice((g * HY_RADIX + r) * LANES, (g * HY_RADIX + r + 1) * LANES)
                y = (acc_ref[:, lanes] * np.float32(scale)
                     + u_ref[0, :, lanes].astype(F32) * skip_ref[:, g * LANES:(g + 1) * LANES])
                tmp_ref[pl.ds(r, rows, stride=HY_RADIX), :] = y * x0_ref[0, :, lanes].astype(F32)
            o_ref[0, :, g * LANES:(g + 1) * LANES] = tmp_ref[...].astype(o_ref.dtype)


def _hyena_long_conv(u, x0c, h, ss, skip):
    bsz, nt_u, width4 = u.shape
    seq_len, width = nt_u * HY_RADIX, width4 // HY_RADIX
    tb = _hy_tables(seq_len)
    nt, nk = tb["nt"], tb["nk"]
    nkt = nk // HY_TK
    tw_spec = lambda im: pl.BlockSpec((3, HY_TK, LANES), im)
    hhat = pl.pallas_call(
        _hy_filter_kernel2,
        grid=(nkt,),
        in_specs=[pl.BlockSpec((2 * HY_TK, nt), lambda i: (i, 0)),
                  _resident((nt, 8 * width), lambda i: (0, 0)),
                  tw_spec(lambda i: (0, i, 0)), tw_spec(lambda i: (0, i, 0)),
                  _resident((1, 2 * width), lambda i: (0, 0))],
        out_specs=pl.BlockSpec((8, HY_TK, width), lambda i: (0, i, 0)),
        out_shape=jax.ShapeDtypeStruct((8, nk, width), F32),
        compiler_params=_params("parallel"),
        name="hyena_filter_spectrum",
    )(tb["f4"], h.reshape(nt, 8 * width), tb["twc"], tb["tws"], ss)
    row4 = (1, nt, 4 * width)
    return pl.pallas_call(
        functools.partial(_hy_conv_kernel, scale=1.0 / seq_len),
        grid=(bsz, nkt),
        in_specs=[pl.BlockSpec((2 * HY_TK, nt), lambda b, i: (i, 0)),
                  pl.BlockSpec((nt, 2 * HY_TK), lambda b, i: (0, i)),
                  pl.BlockSpec(row4, lambda b, i: (b, 0, 0)),
                  pl.BlockSpec((8, HY_TK, width), lambda b, i: (0, i, 0)),
                  tw_spec(lambda b, i: (0, i, 0)), tw_spec(lambda b, i: (0, i, 0)),
                  pl.BlockSpec(row4, lambda b, i: (b, 0, 0), pipeline_mode=pl.Buffered(1)),
                  _resident((1, width), lambda b, i: (0, 0))],
        out_specs=pl.BlockSpec((1, seq_len, width), lambda b, i: (b, 0, 0)),
        out_shape=jax.ShapeDtypeStruct((bsz, seq_len, width), BF16),
        scratch_shapes=[pltpu.VMEM((nt, 4 * width), F32), pltpu.VMEM((seq_len, LANES), F32)],
        compiler_params=_params("parallel", "arbitrary"),
        name="hyena_long_conv",
    )(tb["f4"], tb["g4"], u, hhat, tb["twc"], tb["tws"], x0c, skip.astype(F32).reshape(1, width))


def _sgu_kernel(zu_ref, zv_ref, g_ref, b_ref, w_ref, sb_ref, o_ref):
    tokens, width = zu_ref.shape[1], zu_ref.shape[2]
    u = _gelu(zu_ref[0].astype(F32))
    v = _layer_norm(_gelu(zv_ref[0].astype(F32)), g_ref[...], b_ref[...]).astype(BF16)
    lane = lax.broadcasted_iota(jnp.int32, (SG_CHUNK, LANES), 1)
    first = lane < LANES // 2
    for c in range(tokens // SG_CHUNK):
        rows = slice(c * SG_CHUNK, (c + 1) * SG_CHUNK)
        for p in range(width // LANES):
            cols = slice(p * LANES, (p + 1) * LANES)
            r = jnp.dot(w_ref[p], v[rows, cols], preferred_element_type=F32)
            sv = jnp.where(first, r[:SG_CHUNK], r[SG_CHUNK:]) + sb_ref[:, cols]
            o_ref[0, rows, cols] = (u[rows, cols] * sv).astype(o_ref.dtype)


def _spatial_gating(p3, ln_g, ln_b, sg_w, sg_b, tokens=512):
    bsz, seq_len, _ = p3.shape
    groups = sg_w.shape[0]
    width = ln_g.shape[0]
    gw = width // groups
    assert 2 * gw == LANES and sg_w.shape[1] == SG_CHUNK
    wst = sg_w.astype(BF16).reshape(groups // 2, 2 * SG_CHUNK, SG_CHUNK)
    sb = jnp.repeat(sg_b.astype(F32).T, gw, axis=1)
    blk = (1, tokens, width)
    return pl.pallas_call(
        _sgu_kernel,
        grid=(bsz, seq_len // tokens),
        in_specs=[pl.BlockSpec(blk, lambda b, i: (b, i, 0)),
                  pl.BlockSpec(blk, lambda b, i: (b, i, 1)),
                  _resident((1, width), lambda b, i: (0, 0)),
                  _resident((1, width), lambda b, i: (0, 0)),
                  _resident((groups // 2, 2 * SG_CHUNK, SG_CHUNK), lambda b, i: (0, 0, 0)),
                  _resident((SG_CHUNK, width), lambda b, i: (0, 0))],
        out_specs=pl.BlockSpec(blk, lambda b, i: (b, i, 0)),
        out_shape=jax.ShapeDtypeStruct((bsz, seq_len, width), BF16),
        compiler_params=_params("parallel", "parallel"),
        name="spatial_gating",
    )(p3, p3, ln_g.astype(F32).reshape(1, width), ln_b.astype(F32).reshape(1, width), wst, sb)


CV_PAD = 16
CV_ROWS = 64


def _conf_kernel(za_ref, zg_ref, w_ref, wb_ref, g_ref, b_ref, o_ref, apad_ref):
    seq_len, width = za_ref.shape[1], za_ref.shape[2]
    fill_rows = 256
    pad_tiles = CV_PAD // SUBLANES
    out_tiles = CV_ROWS // SUBLANES
    apad_ref[0:pad_tiles] = jnp.zeros((pad_tiles, SUBLANES, width), F32)
    apad_ref[seq_len // SUBLANES + pad_tiles:seq_len // SUBLANES + 2 * pad_tiles] = jnp.zeros(
        (pad_tiles, SUBLANES, width), F32)

    def fill(i, carry):
        r0 = pl.multiple_of(i * fill_rows, fill_rows)
        a = za_ref[0, pl.ds(r0, fill_rows), :].astype(F32)
        g = zg_ref[0, pl.ds(r0, fill_rows), :].astype(F32)
        apad_ref[pl.ds(i * (fill_rows // SUBLANES) + pad_tiles, fill_rows // SUBLANES)] = (
            (a * jax.nn.sigmoid(g)).reshape(fill_rows // SUBLANES, SUBLANES, width))
        return carry

    lax.fori_loop(0, seq_len // fill_rows, fill, 0)
    sub = lax.broadcasted_iota(jnp.int32, (out_tiles, SUBLANES, width), 1)

    def conv(i, carry):
        r0 = pl.multiple_of(i * CV_ROWS, CV_ROWS)
        off0 = CV_PAD - CV_KERNEL // 2
        acc = None
        for s in range(SUBLANES):
            part = None
            for q in range((off0 + CV_KERNEL - 1) // SUBLANES + 1):
                j = SUBLANES * q + s - off0
                if 0 <= j < CV_KERNEL:
                    term = w_ref[j][None] * apad_ref[pl.ds(i * out_tiles + q, out_tiles + 1)]
                    part = term if part is None else part + term
            if s:
                rot = pltpu.roll(part, SUBLANES - s, 1)
                part = jnp.where(sub < SUBLANES - s, rot[:out_tiles], rot[1:])
            else:
                part = part[:out_tiles]
            acc = part if acc is None else acc + part
        y = _layer_norm(acc + wb_ref[...][None], g_ref[...][None], b_ref[...][None])
        y = y * jax.nn.sigmoid(y)
        o_ref[0, pl.ds(r0, CV_ROWS), :] = y.reshape(CV_ROWS, width).astype(o_ref.dtype)
        return carry

    lax.fori_loop(0, seq_len // CV_ROWS, conv, 0)


def _conformer_conv(p3, lane_block, dw_w, dw_b, ln_g, ln_b):
    bsz, seq_len, _ = p3.shape
    width = dw_w.shape[1]
    blk = (1, seq_len, width)
    vec = lambda a: a.astype(F32).reshape(1, width)
    w8 = jnp.broadcast_to(dw_w.astype(F32)[:, None, :], (CV_KERNEL, SUBLANES, width))
    return pl.pallas_call(
        _conf_kernel,
        grid=(bsz,),
        in_specs=[pl.BlockSpec(blk, lambda b: (b, 0, lane_block)),
                  pl.BlockSpec(blk, lambda b: (b, 0, lane_block + 1)),
                  _resident((CV_KERNEL, SUBLANES, width), lambda b: (0, 0, 0)),
                  _resident((1, width), lambda b: (0, 0)),
                  _resident((1, width), lambda b: (0, 0)),
                  _resident((1, width), lambda b: (0, 0))],
        out_specs=pl.BlockSpec(blk, lambda b: (b, 0, 0)),
        out_shape=jax.ShapeDtypeStruct((bsz, seq_len, width), BF16),
        scratch_shapes=[pltpu.VMEM(((seq_len + 2 * CV_PAD) // SUBLANES, SUBLANES, width), F32)],
        compiler_params=_params("parallel"),
        name="conformer_conv",
    )(p3, p3, w8, vec(dw_b), vec(ln_g), vec(ln_b))


def _out_mlp_kernel(x_ref, a_ref, b_ref, woa_ref, wob_ref, g_ref, w1_ref, w2_ref, gf_ref, o_ref,
                    *, hidden_chunk, final):
    x1 = (x_ref[...]
          + jnp.dot(a_ref[...], woa_ref[...], preferred_element_type=F32)
          + jnp.dot(b_ref[...], wob_ref[...], preferred_element_type=F32))
    hn = _rms(x1, g_ref[...]).astype(BF16)
    mlp = None
    for c in range(w1_ref.shape[1] // hidden_chunk):
        cols = slice(c * hidden_chunk, (c + 1) * hidden_chunk)
        t = jnp.maximum(jnp.dot(hn, w1_ref[:, cols], preferred_element_type=F32), 0.0)
        y = jnp.dot((t * t).astype(BF16), w2_ref[cols, :], preferred_element_type=F32)
        mlp = y if mlp is None else mlp + y
    acc = x1 + mlp
    if final:
        acc = _rms(acc, gf_ref[...])
    o_ref[...] = acc


def _out_mlp(x2, a, b, w_out, g, w1, w2, gf, final, tm=512, hidden_chunk=1024):
    m, d = x2.shape
    half = a.shape[1]
    hidden = w1.shape[1]
    return pl.pallas_call(
        functools.partial(_out_mlp_kernel, hidden_chunk=hidden_chunk, final=final),
        grid=(m // tm,),
        in_specs=[pl.BlockSpec((tm, d), lambda i: (i, 0)),
                  pl.BlockSpec((tm, half), lambda i: (i, 0)),
                  pl.BlockSpec((tm, half), lambda i: (i, 0)),
                  _resident((half, d), lambda i: (0, 0)),
                  _resident((half, d), lambda i: (1, 0)),
                  _resident((1, d), lambda i: (0, 0)),
                  _resident((d, hidden), lambda i: (0, 0)),
                  _resident((hidden, d), lambda i: (0, 0)),
                  _resident((1, d), lambda i: (0, 0))],
        out_specs=pl.BlockSpec((tm, d), lambda i: (i, 0)),
        out_shape=jax.ShapeDtypeStruct((m, d), F32),
        compiler_params=_params("parallel"),
        name="out_mlp",
    )(x2, a, b, w_out, w_out, g.reshape(1, d), w1, w2, gf.reshape(1, d))


def kernel(x, norm_g, ev_w_in, ev_rpb, hy_conv_w, hy_conv_b, hy_w1, hy_b1, hy_w2, hy_b2, hy_w3, hy_b3, hy_freq, hy_w_out, hy_skip, ev_w_out, od_w_in, sg_ln_g, sg_ln_b, sg_w, sg_b, cv_dw_w, cv_dw_b, cv_ln_g, cv_ln_b, od_w_out, mlp_w1, mlp_w2, final_g):
    bsz, seq_len, d = x.shape
    depth = norm_g.shape[0]
    m = bsz * seq_len
    x2 = x.astype(F32).reshape(m, d)
    na_width = ev_rpb.shape[1] * NA_HEAD_DIM
    for i in range(depth):
        j = i // 2
        if i % 2 == 0:
            p3 = _norm_proj(x2, norm_g[i, 0], ev_w_in[j].astype(BF16)).reshape(bsz, seq_len, -1)
            mix_a = _neighbourhood_attention(p3, ev_rpb[j])
            u, x0c = _hyena_pre(p3, 3 * na_width, hy_conv_w[j], hy_conv_b[j])
            h, ss = _hyena_filters(seq_len, hy_w1[j], hy_b1[j], hy_w2[j], hy_b2[j], hy_w3[j], hy_b3[j],
                                   hy_freq[j], hy_w_out[j])
            mix_b = _hyena_long_conv(u, x0c, h, ss, hy_skip[j])
            w_out = ev_w_out[j]
        else:
            p3 = _norm_proj(x2, norm_g[i, 0], od_w_in[j].astype(BF16)).reshape(bsz, seq_len, -1)
            mix_a = _spatial_gating(p3, sg_ln_g[j], sg_ln_b[j], sg_w[j], sg_b[j])
            mix_b = _conformer_conv(p3, 2, cv_dw_w[j], cv_dw_b[j], cv_ln_g[j], cv_ln_b[j])
            w_out = od_w_out[j]
        x2 = _out_mlp(x2, mix_a.reshape(m, -1), mix_b.reshape(m, -1), w_out.astype(BF16), norm_g[i, 1],
                      mlp_w1[i].astype(BF16), mlp_w2[i].astype(BF16), final_g, final=(i == depth - 1))
    return x2.reshape(bsz, seq_len, d).astype(x.dtype)
```

```python
import functools
import math

import numpy as np
import jax
import jax.numpy as jnp
from jax import lax
from jax.experimental import pallas as pl
from jax.experimental.pallas import tpu as pltpu

F32 = jnp.float32
BF16 = jnp.bfloat16
EPS = 1e-6
MASK_VALUE = -1e30

GRID_W = 64
NA_HEAD_DIM = 64
NA_WIN_R = 8
NA_WIN_C = 16
HY_EMB = 33
HY_TARGET = 1e-2
HY_FAST_PCT = 0.3
HY_SLOW_PCT = 1.5
SG_CHUNK = 128
CV_KERNEL = 31

LANES = 128
SUBLANES = 8
VMEM_LIMIT = 56 * 1024 * 1024


def _params(*sem):
    return pltpu.CompilerParams(dimension_semantics=sem, vmem_limit_bytes=VMEM_LIMIT)


def _resident(shape, index_map):
    return pl.BlockSpec(shape, index_map, pipeline_mode=pl.Buffered(1))


def _rms(x, g):
    return x * lax.rsqrt(jnp.mean(x * x, axis=-1, keepdims=True) + EPS) * g


def _layer_norm(x, g, b):
    xc = x - jnp.mean(x, axis=-1, keepdims=True)
    return xc * lax.rsqrt(jnp.mean(xc * xc, axis=-1, keepdims=True) + EPS) * g + b


def _gelu(x):
    return x * (lax.erf(x / np.float32(math.sqrt(2.0))) + 1.0) / 2.0


def _norm_proj_kernel(x_ref, g_ref, w_ref, o_ref):
    hn = _rms(x_ref[...], g_ref[...]).astype(BF16)
    o_ref[...] = jnp.dot(hn, w_ref[...], preferred_element_type=F32).astype(o_ref.dtype)


def _norm_proj(x2, g, w, tm=512):
    m, d = x2.shape
    n = w.shape[1]
    return pl.pallas_call(
        _norm_proj_kernel,
        grid=(m // tm,),
        in_specs=[pl.BlockSpec((tm, d), lambda i: (i, 0)),
                  _resident((1, d), lambda i: (0, 0)),
                  _resident((d, n), lambda i: (0, 0))],
        out_specs=pl.BlockSpec((tm, n), lambda i: (i, 0)),
        out_shape=jax.ShapeDtypeStruct((m, n), BF16),
        compiler_params=_params("parallel"),
        name="norm_proj",
    )(x2, g.reshape(1, d), w)


NA_UNROLL = 8


def _na_kernel(q_ref, k_ref, v_ref, tb_ref, o_ref):
    seq_len = q_ref.shape[1]
    rows = seq_len // GRID_W
    nkeys = NA_WIN_R * GRID_W
    lane = lax.broadcasted_iota(jnp.int32, (GRID_W, LANES), 1)
    first = lane < NA_HEAD_DIM

    def body(i, carry):
        q0s, k0s, cases, scores = [], [], [], []
        for u in range(NA_UNROLL):
            r = i * NA_UNROLL + u
            rs = jnp.clip(r - NA_WIN_R // 2, 0, rows - NA_WIN_R)
            cases.append(r - rs)
            q0s.append(pl.multiple_of(r * GRID_W, GRID_W))
            k0s.append(pl.multiple_of(rs * GRID_W, GRID_W))
            q2 = q_ref[0, pl.ds(q0s[u], GRID_W), :] * BF16(NA_HEAD_DIM ** -0.5)
            k2 = k_ref[0, pl.ds(k0s[u], nkeys), :]
            zero = jnp.zeros_like(q2)
            qs = jnp.concatenate([jnp.where(first, q2, zero), jnp.where(first, zero, q2)], axis=0)
            scores.append(lax.dot_general(qs, k2, (((1,), (1,)), ((), ())), preferred_element_type=F32))
        probs, dens = [], []
        for u in range(NA_UNROLL):
            s = scores[u] + tb_ref[0, cases[u]]
            e = jnp.exp(s - jnp.max(s, axis=-1, keepdims=True))
            dens.append(jnp.sum(e, axis=-1, keepdims=True))
            probs.append(e.astype(BF16))
        outs = [jnp.dot(probs[u], v_ref[0, pl.ds(k0s[u], nkeys), :], preferred_element_type=F32)
                for u in range(NA_UNROLL)]
        for u in range(NA_UNROLL):
            o = outs[u] / dens[u]
            o_ref[0, pl.ds(q0s[u], GRID_W), :] = jnp.where(first, o[:GRID_W], o[GRID_W:]).astype(o_ref.dtype)
        return carry

    lax.fori_loop(0, rows // NA_UNROLL, body, 0)


def _na_bias_table(rpb):
    heads, nrow, ncol = rpb.shape
    qc = np.arange(GRID_W)[:, None]
    kc = np.arange(GRID_W)[None, :]
    cs = np.clip(qc - NA_WIN_C // 2, 0, GRID_W - NA_WIN_C)
    inside = (kc >= cs) & (kc < cs + NA_WIN_C)
    col_off = kc - qc + (NA_WIN_C - 1)
    onehot = (col_off[None] == np.arange(ncol)[:, None, None]) & inside[None]
    expand = jnp.asarray(onehot.reshape(ncol, GRID_W * GRID_W), F32)
    t = jnp.dot(rpb.astype(F32).reshape(heads * nrow, ncol), expand, precision=lax.Precision.HIGHEST)
    t = jnp.where(inside[None, None], t.reshape(heads, nrow, GRID_W, GRID_W), MASK_VALUE)
    t = jnp.stack([t[:, NA_WIN_R - 1 - c:2 * NA_WIN_R - 1 - c] for c in range(NA_WIN_R)], axis=1)
    t = t.transpose(0, 1, 3, 2, 4).reshape(heads // 2, 2, NA_WIN_R, GRID_W, NA_WIN_R * GRID_W)
    return t.transpose(0, 2, 1, 3, 4).reshape(heads // 2, NA_WIN_R, 2 * GRID_W, NA_WIN_R * GRID_W)


def _neighbourhood_attention(p3, rpb):
    bsz, seq_len, _ = p3.shape
    heads = rpb.shape[0]
    pairs = heads // 2
    tb = _na_bias_table(rpb)
    blk = (1, seq_len, LANES)
    return pl.pallas_call(
        _na_kernel,
        grid=(bsz, pairs),
        in_specs=[pl.BlockSpec(blk, lambda b, j: (b, 0, j)),
                  pl.BlockSpec(blk, lambda b, j: (b, 0, pairs + j)),
                  pl.BlockSpec(blk, lambda b, j: (b, 0, 2 * pairs + j)),
                  pl.BlockSpec((1, NA_WIN_R, 2 * GRID_W, NA_WIN_R * GRID_W), lambda b, j: (j, 0, 0, 0))],
        out_specs=pl.BlockSpec(blk, lambda b, j: (b, 0, j)),
        out_shape=jax.ShapeDtypeStruct((bsz, seq_len, pairs * LANES), BF16),
        compiler_params=_params("parallel", "parallel"),
        name="na_attention",
    )(p3, p3, p3, tb)


def _hy_filter_kernel(frq_ref, w1_ref, b1_ref, w2_ref, b2_ref, w3_ref, b3_ref, fq_ref, fo_ref,
                      dl_ref, h_ref, ss_ref, *, seq_len):
    i = pl.program_id(0)
    tl = h_ref.shape[0]
    hp = lax.Precision.HIGHEST
    pos = (lax.broadcasted_iota(jnp.int32, (tl, LANES), 0) + i * tl).astype(F32)
    lane = lax.broadcasted_iota(jnp.int32, (tl, LANES), 1)
    bands = (HY_EMB - 1) // 2
    t = pos / np.float32(seq_len - 1)
    ang = (np.float32(2.0 * math.pi) * pos / np.float32(seq_len)) * frq_ref[...]
    z = jnp.where(lane == 0, t,
                  jnp.where(lane <= bands, jnp.cos(ang),
                            jnp.where(lane <= 2 * bands, -jnp.sin(ang), 0.0)))
    fq = fq_ref[...]
    hid = jnp.sin(fq * (jnp.dot(z, w1_ref[...], precision=hp, preferred_element_type=F32) + b1_ref[...]))
    hid = jnp.sin(fq * (jnp.dot(hid, w2_ref[...], precision=hp, preferred_element_type=F32) + b2_ref[...]))
    hid = jnp.sin(fq * (jnp.dot(hid, w3_ref[...], precision=hp, preferred_element_type=F32) + b3_ref[...]))
    h = jnp.dot(hid, fo_ref[...], precision=hp, preferred_element_type=F32)
    h = h * jnp.exp(-t[:, :1] * dl_ref[...])
    h_ref[...] = h.astype(h_ref.dtype)

    @pl.when(i == 0)
    def _():
        ss_ref[...] = jnp.zeros_like(ss_ref)

    ss_ref[...] += jnp.sum(h * h, axis=0, keepdims=True)


def _pad2(a, rows, cols):
    return jnp.pad(a.astype(F32), ((0, rows - a.shape[0]), (0, cols - a.shape[1])))


def _hyena_filters(seq_len, w1, b1, w2, b2, w3, b3, freq, f_out, tl=512):
    ffn = w1.shape[1]
    width2 = f_out.shape[1]
    bands = (HY_EMB - 1) // 2
    fr = np.linspace(1e-4, bands - 1, bands, dtype=np.float32)
    frq = np.zeros((1, LANES), np.float32)
    frq[0, 1:1 + bands] = fr
    frq[0, 1 + bands:1 + 2 * bands] = fr
    deltas = np.abs(np.linspace(math.log(HY_TARGET) / HY_SLOW_PCT, math.log(HY_TARGET) / HY_FAST_PCT,
                                width2 // 2, dtype=np.float32))
    dl = np.tile(deltas, 2)[None, :]
    row = lambda a: _pad2(a.reshape(1, -1), 1, LANES)
    small = lambda shape: _resident(shape, lambda i: (0, 0))
    return pl.pallas_call(
        functools.partial(_hy_filter_kernel, seq_len=seq_len),
        grid=(seq_len // tl,),
        in_specs=[small((1, LANES)), small((LANES, LANES)), small((1, LANES)), small((LANES, LANES)),
                  small((1, LANES)), small((LANES, LANES)), small((1, LANES)), small((1, LANES)),
                  small((LANES, width2)), small((1, width2))],
        out_specs=[pl.BlockSpec((tl, width2), lambda i: (i, 0)),
                   pl.BlockSpec((1, width2), lambda i: (0, 0))],
        out_shape=[jax.ShapeDtypeStruct((seq_len, width2), BF16),
                   jax.ShapeDtypeStruct((1, width2), F32)],
        compiler_params=_params("arbitrary"),
        name="hyena_filters",
    )(jnp.asarray(frq), _pad2(w1, LANES, LANES), row(b1), _pad2(w2, LANES, LANES), row(b2),
      _pad2(w3, LANES, LANES), row(b3), row(freq), _pad2(f_out, LANES, width2), jnp.asarray(dl))


HY_TK = 256


def _hy_tables(seq_len):
    n = 2 * seq_len
    nt, nk = seq_len // 4, n // 8
    kap = jnp.arange(nk, dtype=jnp.int32)[:, None]
    tau = jnp.arange(nt, dtype=jnp.int32)[None, :]
    phase = ((2 * kap + 1) * tau) % (n // 2)
    ang = phase.astype(F32) * np.float32(2.0 * math.pi / (n // 2))
    c = jnp.cos(ang).reshape(nk // HY_TK, 1, HY_TK, nt)
    s = (-jnp.sin(ang)).reshape(nk // HY_TK, 1, HY_TK, nt)
    f4 = jnp.concatenate([c, s], axis=1).reshape(2 * nk, nt).astype(BF16)
    rho = jnp.arange(1, 4, dtype=jnp.int32)[:, None]
    th = ((2 * kap.reshape(1, nk) + 1) * rho).astype(F32) * np.float32(2.0 * math.pi / (2 * n))
    lanes = lambda a: jnp.broadcast_to(a[:, :, None], (3, nk, LANES))
    return dict(f4=f4, g4=f4.T, twc=lanes(jnp.cos(th)), tws=lanes(jnp.sin(th)), nt=nt, nk=nk)


HY_RADIX = 4


def _hy_pre_kernel(x0_ref, x1_ref, v_ref, w_ref, b_ref, u_ref, x0c_ref, tmp_ref):
    seq_len = x0_ref.shape[1]
    row = lax.broadcasted_iota(jnp.int32, (seq_len, LANES), 0)

    def conv(z_ref, i):
        z = z_ref[0].astype(F32)
        zm = jnp.where(row == 0, 0.0, pltpu.roll(z, 1, 0))
        zp = jnp.where(row == seq_len - 1, 0.0, pltpu.roll(z, seq_len - 1, 0))
        w = w_ref[i]
        return w[0:1] * zm + w[1:2] * z + w[2:3] * zp + b_ref[i]

    def store_merged(o_ref, val):
        tmp_ref[...] = val
        for r in range(HY_RADIX):
            o_ref[0, :, r * LANES:(r + 1) * LANES] = tmp_ref[pl.ds(r, seq_len // HY_RADIX, stride=HY_RADIX), :].astype(o_ref.dtype)

    store_merged(x0c_ref, conv(x0_ref, 0))
    store_merged(u_ref, conv(v_ref, 2) * conv(x1_ref, 1))


def _hyena_pre(p3, lane0, conv_w, conv_b):
    bsz, seq_len, _ = p3.shape
    width = conv_w.shape[1] // 3
    nt = width // LANES
    off = lane0 // LANES
    w3 = conv_w.astype(F32).reshape(conv_w.shape[0], 3, width).transpose(1, 0, 2)
    b3 = conv_b.astype(F32).reshape(3, 1, width)
    blk = (1, seq_len, LANES)
    oblk = (1, seq_len // HY_RADIX, HY_RADIX * LANES)
    out = jax.ShapeDtypeStruct((bsz, seq_len // HY_RADIX, HY_RADIX * width), BF16)
    return pl.pallas_call(
        _hy_pre_kernel,
        grid=(bsz, nt),
        in_specs=[pl.BlockSpec(blk, lambda b, j: (b, 0, off + j)),
                  pl.BlockSpec(blk, lambda b, j: (b, 0, off + nt + j)),
                  pl.BlockSpec(blk, lambda b, j: (b, 0, off + 2 * nt + j)),
                  pl.BlockSpec((3, conv_w.shape[0], LANES), lambda b, j: (0, 0, j)),
                  pl.BlockSpec((3, 1, LANES), lambda b, j: (0, 0, j))],
        out_specs=[pl.BlockSpec(oblk, lambda b, j: (b, 0, j)),
                   pl.BlockSpec(oblk, lambda b, j: (b, 0, j))],
        out_shape=[out, out],
        scratch_shapes=[pltpu.VMEM((seq_len, LANES), F32)],
        compiler_params=_params("parallel", "parallel"),
        name="hyena_pre",
    )(p3, p3, p3, w3, b3)


def _lane_tile(a, width):
    return jnp.concatenate([a] * (width // a.shape[1]), axis=1)


def _rho_slice(a, r, width, chunk):
    return jnp.concatenate([a[:, (g * HY_RADIX + r) * chunk:(g * HY_RADIX + r + 1) * chunk]
                            for g in range(width // chunk)], axis=1)


def _hy_bins(q, twc_ref, tws_ref, width, chunk):
    tk = q.shape[0] // 2
    qr, qi = q[:tk], q[tk:]
    zr, zi = [_rho_slice(qr, 0, width, chunk)], [_rho_slice(qi, 0, width, chunk)]
    for r in range(1, HY_RADIX):
        c = _lane_tile(twc_ref[r - 1], width)
        s = _lane_tile(tws_ref[r - 1], width)
        a, b = _rho_slice(qr, r, width, chunk), _rho_slice(qi, r, width, chunk)
        zr.append(a * c + b * s)
        zi.append(b * c - a * s)
    t0r, t0i = zr[0] + zr[2], zi[0] + zi[2]
    t1r, t1i = zr[0] - zr[2], zi[0] - zi[2]
    t2r, t2i = zr[1] + zr[3], zi[1] + zi[3]
    t3r, t3i = zr[1] - zr[3], zi[1] - zi[3]
    return ((t0r + t2r, t0i + t2i), (t1r + t3i, t1i - t3r), (t1r - t3i, -t1i - t3r), (t0r - t2r, t2i - t0i))


def _hy_filter_kernel2(f_ref, h_ref, twc_ref, tws_ref, ss_ref, o_ref):
    width = o_ref.shape[2]
    q = jnp.dot(f_ref[...], h_ref[...], preferred_element_type=F32)
    ss = ss_ref[...]
    scale = lax.rsqrt(ss[:, :width] + ss[:, width:])
    for t, (xr, xi) in enumerate(_hy_bins(q, twc_ref, tws_ref, 2 * width, 2 * width)):
        o_ref[2 * t] = (xr[:, :width] + xr[:, width:]) * scale
        o_ref[2 * t + 1] = (xi[:, :width] - xi[:, width:]) * scale


def _hy_conv_kernel(f_ref, g_ref, u_ref, h_ref, twc_ref, tws_ref, x0_ref, skip_ref, o_ref, acc_ref, tmp_ref, *, scale):
    kt = pl.program_id(1)
    width = h_ref.shape[2]
    q = jnp.dot(f_ref[...], u_ref[0], preferred_element_type=F32)
    prod = []
    for t, (xr, xi) in enumerate(_hy_bins(q, twc_ref, tws_ref, width, LANES)):
        hr, hi = h_ref[2 * t], h_ref[2 * t + 1]
        prod.append((xr * hr - xi * hi, xr * hi + xi * hr))
    (yar, yai), (ybr, ybi), (ycr, yci), (ydr, ydi) = prod
    e0r, e0i = yar + ydr, yai - ydi
    e1r, e1i = yar - ydr, yai + ydi
    e2r, e2i = ybr + ycr, ybi - yci
    e3r, e3i = ybr - ycr, ybi + yci
    sr = [e0r + e2r, e1r - e3i, e0r - e2r, e1r + e3i]
    si = [e0i + e2i, e1i + e3r, e0i - e2i, e1i - e3r]
    vr, vi = [sr[0]], [si[0]]
    for r in range(1, HY_RADIX):
        c = _lane_tile(twc_ref[r - 1], width)
        s = _lane_tile(tws_ref[r - 1], width)
        vr.append(sr[r] * c - si[r] * s)
        vi.append(sr[r] * s + si[r] * c)
    merged = lambda parts: jnp.concatenate([parts[r][:, g * LANES:(g + 1) * LANES]
                                            for g in range(width // LANES) for r in range(HY_RADIX)], axis=1)
    v = jnp.concatenate([merged(vr), merged(vi)], axis=0).astype(BF16)
    part = jnp.dot(g_ref[...], v, preferred_element_type=F32)

    @pl.when(kt == 0)
    def _():
        acc_ref[...] = part

    @pl.when(kt > 0)
    def _():
        acc_ref[...] += part

    @pl.when(kt == pl.num_programs(1) - 1)
    def _():
        rows = acc_ref.shape[0]
        for g in range(width // LANES):
            for r in range(HY_RADIX):
                lanes = slice((g * HY_RADIX + r) * LANES, (g * HY_RADIX + r + 1) * LANES)
                y = (acc_ref[:, lanes] * np.float32(scale)
                     + u_ref[0, :, lanes].astype(F32) * skip_ref[:, g * LANES:(g + 1) * LANES])
                tmp_ref[pl.ds(r, rows, stride=HY_RADIX), :] = y * x0_ref[0, :, lanes].astype(F32)
            o_ref[0, :, g * LANES:(g + 1) * LANES] = tmp_ref[...].astype(o_ref.dtype)


def _hyena_long_conv(u, x0c, h, ss, skip):
    bsz, nt_u, width4 = u.shape
    seq_len, width = nt_u * HY_RADIX, width4 // HY_RADIX
    tb = _hy_tables(seq_len)
    nt, nk = tb["nt"], tb["nk"]
    nkt = nk // HY_TK
    tw_spec = lambda im: pl.BlockSpec((3, HY_TK, LANES), im)
    hhat = pl.pallas_call(
        _hy_filter_kernel2,
        grid=(nkt,),
        in_specs=[pl.BlockSpec((2 * HY_TK, nt), lambda i: (i, 0)),
                  _resident((nt, 8 * width), lambda i: (0, 0)),
                  tw_spec(lambda i: (0, i, 0)), tw_spec(lambda i: (0, i, 0)),
                  _resident((1, 2 * width), lambda i: (0, 0))],
        out_specs=pl.BlockSpec((8, HY_TK, width), lambda i: (0, i, 0)),
        out_shape=jax.ShapeDtypeStruct((8, nk, width), F32),
        compiler_params=_params("parallel"),
        name="hyena_filter_spectrum",
    )(tb["f4"], h.reshape(nt, 8 * width), tb["twc"], tb["tws"], ss)
    row4 = (1, nt, 4 * width)
    return pl.pallas_call(
        functools.partial(_hy_conv_kernel, scale=1.0 / seq_len),
        grid=(bsz, nkt),
        in_specs=[pl.BlockSpec((2 * HY_TK, nt), lambda b, i: (i, 0)),
                  pl.BlockSpec((nt, 2 * HY_TK), lambda b, i: (0, i)),
                  pl.BlockSpec(row4, lambda b, i: (b, 0, 0)),
                  pl.BlockSpec((8, HY_TK, width), lambda b, i: (0, i, 0)),
                  tw_spec(lambda b, i: (0, i, 0)), tw_spec(lambda b, i: (0, i, 0)),
                  pl.BlockSpec(row4, lambda b, i: (b, 0, 0), pipeline_mode=pl.Buffered(1)),
                  _resident((1, width), lambda b, i: (0, 0))],
        out_specs=pl.BlockSpec((1, seq_len, width), lambda b, i: (b, 0, 0)),
        out_shape=jax.ShapeDtypeStruct((bsz, seq_len, width), BF16),
        scratch_shapes=[pltpu.VMEM((nt, 4 * width), F32), pltpu.VMEM((seq_len, LANES), F32)],
        compiler_params=_params("parallel", "arbitrary"),
        name="hyena_long_conv",
    )(tb["f4"], tb["g4"], u, hhat, tb["twc"], tb["tws"], x0c, skip.astype(F32).reshape(1, width))


def _sgu_kernel(zu_ref, zv_ref, g_ref, b_ref, w_ref, sb_ref, o_ref):
    tokens, width = zu_ref.shape[1], zu_ref.shape[2]
    u = _gelu(zu_ref[0].astype(F32))
    v = _layer_norm(_gelu(zv_ref[0].astype(F32)), g_ref[...], b_ref[...]).astype(BF16)
    lane = lax.broadcasted_iota(jnp.int32, (SG_CHUNK, LANES), 1)
    first = lane < LANES // 2
    for c in range(tokens // SG_CHUNK):
        rows = slice(c * SG_CHUNK, (c + 1) * SG_CHUNK)
        for p in range(width // LANES):
            cols = slice(p * LANES, (p + 1) * LANES)
            r = jnp.dot(w_ref[p], v[rows, cols], preferred_element_type=F32)
            sv = jnp.where(first, r[:SG_CHUNK], r[SG_CHUNK:]) + sb_ref[:, cols]
            o_ref[0, rows, cols] = (u[rows, cols] * sv).astype(o_ref.dtype)


def _spatial_gating(p3, ln_g, ln_b, sg_w, sg_b, tokens=512):
    bsz, seq_len, _ = p3.shape
    groups = sg_w.shape[0]
    width = ln_g.shape[0]
    gw = width // groups
    assert 2 * gw == LANES and sg_w.shape[1] == SG_CHUNK
    wst = sg_w.astype(BF16).reshape(groups // 2, 2 * SG_CHUNK, SG_CHUNK)
    sb = jnp.repeat(sg_b.astype(F32).T, gw, axis=1)
    blk = (1, tokens, width)
    return pl.pallas_call(
        _sgu_kernel,
        grid=(bsz, seq_len // tokens),
        in_specs=[pl.BlockSpec(blk, lambda b, i: (b, i, 0)),
                  pl.BlockSpec(blk, lambda b, i: (b, i, 1)),
                  _resident((1, width), lambda b, i: (0, 0)),
                  _resident((1, width), lambda b, i: (0, 0)),
                  _resident((groups // 2, 2 * SG_CHUNK, SG_CHUNK), lambda b, i: (0, 0, 0)),
                  _resident((SG_CHUNK, width), lambda b, i: (0, 0))],
        out_specs=pl.BlockSpec(blk, lambda b, i: (b, i, 0)),
        out_shape=jax.ShapeDtypeStruct((bsz, seq_len, width), BF16),
        compiler_params=_params("parallel", "parallel"),
        name="spatial_gating",
    )(p3, p3, ln_g.astype(F32).reshape(1, width), ln_b.astype(F32).reshape(1, width), wst, sb)


CV_PAD = 16
CV_ROWS = 64
CV_UNROLL = 4


def _conf_kernel(za_ref, zg_ref, w_ref, wb_ref, g_ref, b_ref, o_ref, apad_ref):
    seq_len, width = za_ref.shape[1], za_ref.shape[2]
    fill_rows = 256
    pad_tiles = CV_PAD // SUBLANES
    out_tiles = CV_ROWS // SUBLANES
    apad_ref[0:pad_tiles] = jnp.zeros((pad_tiles, SUBLANES, width), F32)
    apad_ref[seq_len // SUBLANES + pad_tiles:seq_len // SUBLANES + 2 * pad_tiles] = jnp.zeros(
        (pad_tiles, SUBLANES, width), F32)

    def fill(i, carry):
        r0 = pl.multiple_of(i * fill_rows, fill_rows)
        a = za_ref[0, pl.ds(r0, fill_rows), :].astype(F32)
        g = zg_ref[0, pl.ds(r0, fill_rows), :].astype(F32)
        apad_ref[pl.ds(i * (fill_rows // SUBLANES) + pad_tiles, fill_rows // SUBLANES)] = (
            (a * jax.nn.sigmoid(g)).reshape(fill_rows // SUBLANES, SUBLANES, width))
        return carry

    lax.fori_loop(0, seq_len // fill_rows, fill, 0)
    sub = lax.broadcasted_iota(jnp.int32, (out_tiles, SUBLANES, width), 1)

    def conv(i, carry):
        r0 = pl.multiple_of(i * CV_ROWS, CV_ROWS)
        off0 = CV_PAD - CV_KERNEL // 2
        acc = None
        for s in range(SUBLANES):
            part = None
            for q in range((off0 + CV_KERNEL - 1) // SUBLANES + 1):
                j = SUBLANES * q + s - off0
                if 0 <= j < CV_KERNEL:
                    term = w_ref[j][None] * apad_ref[pl.ds(i * out_tiles + q, out_tiles + 1)]
                    part = term if part is None else part + term
            if s:
                rot = pltpu.roll(part, SUBLANES - s, 1)
                part = jnp.where(sub < SUBLANES - s, rot[:out_tiles], rot[1:])
            else:
                part = part[:out_tiles]
            acc = part if acc is None else acc + part
        y = _layer_norm(acc + wb_ref[...][None], g_ref[...][None], b_ref[...][None])
        y = y * jax.nn.sigmoid(y)
        o_ref[0, pl.ds(r0, CV_ROWS), :] = y.reshape(CV_ROWS, width).astype(o_ref.dtype)
        return carry

    lax.fori_loop(0, seq_len // CV_ROWS, conv, 0, unroll=CV_UNROLL)


def _conformer_conv(p3, lane_block, dw_w, dw_b, ln_g, ln_b):
    bsz, seq_len, _ = p3.shape
    width = dw_w.shape[1]
    blk = (1, seq_len, width)
    vec = lambda a: a.astype(F32).reshape(1, width)
    w8 = jnp.broadcast_to(dw_w.astype(F32)[:, None, :], (CV_KERNEL, SUBLANES, width))
    return pl.pallas_call(
        _conf_kernel,
        grid=(bsz,),
        in_specs=[pl.BlockSpec(blk, lambda b: (b, 0, lane_block)),
                  pl.BlockSpec(blk, lambda b: (b, 0, lane_block + 1)),
                  _resident((CV_KERNEL, SUBLANES, width), lambda b: (0, 0, 0)),
                  _resident((1, width), lambda b: (0, 0)),
                  _resident((1, width), lambda b: (0, 0)),
                  _resident((1, width), lambda b: (0, 0))],
        out_specs=pl.BlockSpec(blk, lambda b: (b, 0, 0)),
        out_shape=jax.ShapeDtypeStruct((bsz, seq_len, width), BF16),
        scratch_shapes=[pltpu.VMEM(((seq_len + 2 * CV_PAD) // SUBLANES, SUBLANES, width), F32)],
        compiler_params=_params("parallel"),
        name="conformer_conv",
    )(p3, p3, w8, vec(dw_b), vec(ln_g), vec(ln_b))


def _out_mlp_kernel(x_ref, a_ref, b_ref, woa_ref, wob_ref, g_ref, w1_ref, w2_ref, gf_ref, o_ref,
                    *, hidden_chunk, final):
    x1 = (x_ref[...]
          + jnp.dot(a_ref[...], woa_ref[...], preferred_element_type=F32)
          + jnp.dot(b_ref[...], wob_ref[...], preferred_element_type=F32))
    hn = _rms(x1, g_ref[...]).astype(BF16)
    mlp = None
    for c in range(w1_ref.shape[1] // hidden_chunk):
        cols = slice(c * hidden_chunk, (c + 1) * hidden_chunk)
        t = jnp.maximum(jnp.dot(hn, w1_ref[:, cols], preferred_element_type=F32), 0.0)
        y = jnp.dot((t * t).astype(BF16), w2_ref[cols, :], preferred_element_type=F32)
        mlp = y if mlp is None else mlp + y
    acc = x1 + mlp
    if final:
        acc = _rms(acc, gf_ref[...])
    o_ref[...] = acc


def _out_mlp(x2, a, b, w_out, g, w1, w2, gf, final, tm=512, hidden_chunk=1024):
    m, d = x2.shape
    half = a.shape[1]
    hidden = w1.shape[1]
    return pl.pallas_call(
        functools.partial(_out_mlp_kernel, hidden_chunk=hidden_chunk, final=final),
        grid=(m // tm,),
        in_specs=[pl.BlockSpec((tm, d), lambda i: (i, 0)),
                  pl.BlockSpec((tm, half), lambda i: (i, 0)),
                  pl.BlockSpec((tm, half), lambda i: (i, 0)),
                  _resident((half, d), lambda i: (0, 0)),
                  _resident((half, d), lambda i: (1, 0)),
                  _resident((1, d), lambda i: (0, 0)),
                  _resident((d, hidden), lambda i: (0, 0)),
                  _resident((hidden, d), lambda i: (0, 0)),
                  _resident((1, d), lambda i: (0, 0))],
        out_specs=pl.BlockSpec((tm, d), lambda i: (i, 0)),
        out_shape=jax.ShapeDtypeStruct((m, d), F32),
        compiler_params=_params("parallel"),
        name="out_mlp",
    )(x2, a, b, w_out, w_out, g.reshape(1, d), w1, w2, gf.reshape(1, d))


def kernel(x, norm_g, ev_w_in, ev_rpb, hy_conv_w, hy_conv_b, hy_w1, hy_b1, hy_w2, hy_b2, hy_w3, hy_b3, hy_freq, hy_w_out, hy_skip, ev_w_out, od_w_in, sg_ln_g, sg_ln_b, sg_w, sg_b, cv_dw_w, cv_dw_b, cv_ln_g, cv_ln_b, od_w_out, mlp_w1, mlp_w2, final_g):
    bsz, seq_len, d = x.shape
    depth = norm_g.shape[0]
    m = bsz * seq_len
    x2 = x.astype(F32).reshape(m, d)
    na_width = ev_rpb.shape[1] * NA_HEAD_DIM
    for i in range(depth):
        j = i // 2
        if i % 2 == 0:
            p3 = _norm_proj(x2, norm_g[i, 0], ev_w_in[j].astype(BF16)).reshape(bsz, seq_len, -1)
            mix_a = _neighbourhood_attention(p3, ev_rpb[j])
            u, x0c = _hyena_pre(p3, 3 * na_width, hy_conv_w[j], hy_conv_b[j])
            h, ss = _hyena_filters(seq_len, hy_w1[j], hy_b1[j], hy_w2[j], hy_b2[j], hy_w3[j], hy_b3[j],
                                   hy_freq[j], hy_w_out[j])
            mix_b = _hyena_long_conv(u, x0c, h, ss, hy_skip[j])
            w_out = ev_w_out[j]
        else:
            p3 = _norm_proj(x2, norm_g[i, 0], od_w_in[j].astype(BF16)).reshape(bsz, seq_len, -1)
            mix_a = _spatial_gating(p3, sg_ln_g[j], sg_ln_b[j], sg_w[j], sg_b[j])
            mix_b = _conformer_conv(p3, 2, cv_dw_w[j], cv_dw_b[j], cv_ln_g[j], cv_ln_b[j])
            w_out = od_w_out[j]
        x2 = _out_mlp(x2, mix_a.reshape(m, -1), mix_b.reshape(m, -1), w_out.astype(BF16), norm_g[i, 1],
                      mlp_w1[i].astype(BF16), mlp_w2[i].astype(BF16), final_g, final=(i == depth - 1))
    return x2.reshape(bsz, seq_len, d).astype(x.dtype)
```

```python
import functools
import math

import numpy as np
import jax
import jax.numpy as jnp
from jax import lax
from jax.experimental import pallas as pl
from jax.experimental.pallas import tpu as pltpu

F32 = jnp.float32
BF16 = jnp.bfloat16
EPS = 1e-6
MASK_VALUE = -1e30

GRID_W = 64
NA_HEAD_DIM = 64
NA_WIN_R = 8
NA_WIN_C = 16
HY_EMB = 33
HY_TARGET = 1e-2
HY_FAST_PCT = 0.3
HY_SLOW_PCT = 1.5
SG_CHUNK = 128
CV_KERNEL = 31

LANES = 128
SUBLANES = 8
VMEM_LIMIT = 56 * 1024 * 1024


def _params(*sem):
    return pltpu.CompilerParams(dimension_semantics=sem, vmem_limit_bytes=VMEM_LIMIT)


def _resident(shape, index_map):
    return pl.BlockSpec(shape, index_map, pipeline_mode=pl.Buffered(1))


def _rms(x, g):
    return x * lax.rsqrt(jnp.mean(x * x, axis=-1, keepdims=True) + EPS) * g


def _layer_norm(x, g, b):
    xc = x - jnp.mean(x, axis=-1, keepdims=True)
    return xc * lax.rsqrt(jnp.mean(xc * xc, axis=-1, keepdims=True) + EPS) * g + b


def _gelu(x):
    return x * (lax.erf(x / np.float32(math.sqrt(2.0))) + 1.0) / 2.0


def _norm_proj_kernel(x_ref, g_ref, w_ref, o_ref):
    hn = _rms(x_ref[...], g_ref[...]).astype(BF16)
    o_ref[...] = jnp.dot(hn, w_ref[...], preferred_element_type=F32).astype(o_ref.dtype)


def _norm_proj(x2, g, w, tm=1024):
    m, d = x2.shape
    n = w.shape[1]
    return pl.pallas_call(
        _norm_proj_kernel,
        grid=(m // tm,),
        in_specs=[pl.BlockSpec((tm, d), lambda i: (i, 0)),
                  _resident((1, d), lambda i: (0, 0)),
                  _resident((d, n), lambda i: (0, 0))],
        out_specs=pl.BlockSpec((tm, n), lambda i: (i, 0)),
        out_shape=jax.ShapeDtypeStruct((m, n), BF16),
        compiler_params=_params("parallel"),
        name="norm_proj",
    )(x2, g.reshape(1, d), w)


NA_UNROLL = 8


def _na_kernel(q_ref, k_ref, v_ref, tb_ref, o_ref):
    seq_len = q_ref.shape[1]
    rows = seq_len // GRID_W
    nkeys = NA_WIN_R * GRID_W
    lane = lax.broadcasted_iota(jnp.int32, (GRID_W, LANES), 1)
    first = lane < NA_HEAD_DIM

    def body(i, carry):
        q0s, k0s, cases, scores = [], [], [], []
        for u in range(NA_UNROLL):
            r = i * NA_UNROLL + u
            rs = jnp.clip(r - NA_WIN_R // 2, 0, rows - NA_WIN_R)
            cases.append(r - rs)
            q0s.append(pl.multiple_of(r * GRID_W, GRID_W))
            k0s.append(pl.multiple_of(rs * GRID_W, GRID_W))
            q2 = q_ref[0, pl.ds(q0s[u], GRID_W), :] * BF16(NA_HEAD_DIM ** -0.5)
            k2 = k_ref[0, pl.ds(k0s[u], nkeys), :]
            zero = jnp.zeros_like(q2)
            qs = jnp.concatenate([jnp.where(first, q2, zero), jnp.where(first, zero, q2)], axis=0)
            scores.append(lax.dot_general(qs, k2, (((1,), (1,)), ((), ())), preferred_element_type=F32))
        probs, dens = [], []
        for u in range(NA_UNROLL):
            s = scores[u] + tb_ref[0, cases[u]]
            e = jnp.exp(s - jnp.max(s, axis=-1, keepdims=True))
            dens.append(jnp.sum(e, axis=-1, keepdims=True))
            probs.append(e.astype(BF16))
        outs = [jnp.dot(probs[u], v_ref[0, pl.ds(k0s[u], nkeys), :], preferred_element_type=F32)
                for u in range(NA_UNROLL)]
        for u in range(NA_UNROLL):
            o = outs[u] / dens[u]
            o_ref[0, pl.ds(q0s[u], GRID_W), :] = jnp.where(first, o[:GRID_W], o[GRID_W:]).astype(o_ref.dtype)
        return carry

    lax.fori_loop(0, rows // NA_UNROLL, body, 0)


def _na_bias_table(rpb):
    heads, nrow, ncol = rpb.shape
    qc = np.arange(GRID_W)[:, None]
    kc = np.arange(GRID_W)[None, :]
    cs = np.clip(qc - NA_WIN_C // 2, 0, GRID_W - NA_WIN_C)
    inside = (kc >= cs) & (kc < cs + NA_WIN_C)
    col_off = kc - qc + (NA_WIN_C - 1)
    onehot = (col_off[None] == np.arange(ncol)[:, None, None]) & inside[None]
    expand = jnp.asarray(onehot.reshape(ncol, GRID_W * GRID_W), F32)
    t = jnp.dot(rpb.astype(F32).reshape(heads * nrow, ncol), expand, precision=lax.Precision.HIGHEST)
    t = jnp.where(inside[None, None], t.reshape(heads, nrow, GRID_W, GRID_W), MASK_VALUE)
    t = jnp.stack([t[:, NA_WIN_R - 1 - c:2 * NA_WIN_R - 1 - c] for c in range(NA_WIN_R)], axis=1)
    t = t.transpose(0, 1, 3, 2, 4).reshape(heads // 2, 2, NA_WIN_R, GRID_W, NA_WIN_R * GRID_W)
    return t.transpose(0, 2, 1, 3, 4).reshape(heads // 2, NA_WIN_R, 2 * GRID_W, NA_WIN_R * GRID_W)


def _neighbourhood_attention(p3, rpb):
    bsz, seq_len, _ = p3.shape
    heads = rpb.shape[0]
    pairs = heads // 2
    tb = _na_bias_table(rpb)
    blk = (1, seq_len, LANES)
    return pl.pallas_call(
        _na_kernel,
        grid=(bsz, pairs),
        in_specs=[pl.BlockSpec(blk, lambda b, j: (b, 0, j)),
                  pl.BlockSpec(blk, lambda b, j: (b, 0, pairs + j)),
                  pl.BlockSpec(blk, lambda b, j: (b, 0, 2 * pairs + j)),
                  pl.BlockSpec((1, NA_WIN_R, 2 * GRID_W, NA_WIN_R * GRID_W), lambda b, j: (j, 0, 0, 0))],
        out_specs=pl.BlockSpec(blk, lambda b, j: (b, 0, j)),
        out_shape=jax.ShapeDtypeStruct((bsz, seq_len, pairs * LANES), BF16),
        compiler_params=_params("parallel", "parallel"),
        name="na_attention",
    )(p3, p3, p3, tb)


def _hy_filter_kernel(frq_ref, w1_ref, b1_ref, w2_ref, b2_ref, w3_ref, b3_ref, fq_ref, fo_ref,
                      dl_ref, h_ref, ss_ref, *, seq_len):
    i = pl.program_id(0)
    tl = h_ref.shape[0]
    hp = lax.Precision.HIGHEST
    pos = (lax.broadcasted_iota(jnp.int32, (tl, LANES), 0) + i * tl).astype(F32)
    lane = lax.broadcasted_iota(jnp.int32, (tl, LANES), 1)
    bands = (HY_EMB - 1) // 2
    t = pos / np.float32(seq_len - 1)
    ang = (np.float32(2.0 * math.pi) * pos / np.float32(seq_len)) * frq_ref[...]
    z = jnp.where(lane == 0, t,
                  jnp.where(lane <= bands, jnp.cos(ang),
                            jnp.where(lane <= 2 * bands, -jnp.sin(ang), 0.0)))
    fq = fq_ref[...]
    hid = jnp.sin(fq * (jnp.dot(z, w1_ref[...], precision=hp, preferred_element_type=F32) + b1_ref[...]))
    hid = jnp.sin(fq * (jnp.dot(hid, w2_ref[...], precision=hp, preferred_element_type=F32) + b2_ref[...]))
    hid = jnp.sin(fq * (jnp.dot(hid, w3_ref[...], precision=hp, preferred_element_type=F32) + b3_ref[...]))
    h = jnp.dot(hid, fo_ref[...], precision=hp, preferred_element_type=F32)
    h = h * jnp.exp(-t[:, :1] * dl_ref[...])
    h_ref[...] = h.astype(h_ref.dtype)

    @pl.when(i == 0)
    def _():
        ss_ref[...] = jnp.zeros_like(ss_ref)

    ss_ref[...] += jnp.sum(h * h, axis=0, keepdims=True)


def _pad2(a, rows, cols):
    return jnp.pad(a.astype(F32), ((0, rows - a.shape[0]), (0, cols - a.shape[1])))


def _hyena_filters(seq_len, w1, b1, w2, b2, w3, b3, freq, f_out, tl=512):
    ffn = w1.shape[1]
    width2 = f_out.shape[1]
    bands = (HY_EMB - 1) // 2
    fr = np.linspace(1e-4, bands - 1, bands, dtype=np.float32)
    frq = np.zeros((1, LANES), np.float32)
    frq[0, 1:1 + bands] = fr
    frq[0, 1 + bands:1 + 2 * bands] = fr
    deltas = np.abs(np.linspace(math.log(HY_TARGET) / HY_SLOW_PCT, math.log(HY_TARGET) / HY_FAST_PCT,
                                width2 // 2, dtype=np.float32))
    dl = np.tile(deltas, 2)[None, :]
    row = lambda a: _pad2(a.reshape(1, -1), 1, LANES)
    small = lambda shape: _resident(shape, lambda i: (0, 0))
    return pl.pallas_call(
        functools.partial(_hy_filter_kernel, seq_len=seq_len),
        grid=(seq_len // tl,),
        in_specs=[small((1, LANES)), small((LANES, LANES)), small((1, LANES)), small((LANES, LANES)),
                  small((1, LANES)), small((LANES, LANES)), small((1, LANES)), small((1, LANES)),
                  small((LANES, width2)), small((1, width2))],
        out_specs=[pl.BlockSpec((tl, width2), lambda i: (i, 0)),
                   pl.BlockSpec((1, width2), lambda i: (0, 0))],
        out_shape=[jax.ShapeDtypeStruct((seq_len, width2), BF16),
                   jax.ShapeDtypeStruct((1, width2), F32)],
        compiler_params=_params("arbitrary"),
        name="hyena_filters",
    )(jnp.asarray(frq), _pad2(w1, LANES, LANES), row(b1), _pad2(w2, LANES, LANES), row(b2),
      _pad2(w3, LANES, LANES), row(b3), row(freq), _pad2(f_out, LANES, width2), jnp.asarray(dl))


HY_TK = 256


def _hy_tables(seq_len):
    n = 2 * seq_len
    nt, nk = seq_len // 4, n // 8
    kap = jnp.arange(nk, dtype=jnp.int32)[:, None]
    tau = jnp.arange(nt, dtype=jnp.int32)[None, :]
    phase = ((2 * kap + 1) * tau) % (n // 2)
    ang = phase.astype(F32) * np.float32(2.0 * math.pi / (n // 2))
    c = jnp.cos(ang).reshape(nk // HY_TK, 1, HY_TK, nt)
    s = (-jnp.sin(ang)).reshape(nk // HY_TK, 1, HY_TK, nt)
    f4 = jnp.concatenate([c, s], axis=1).reshape(2 * nk, nt).astype(BF16)
    rho = jnp.arange(1, 4, dtype=jnp.int32)[:, None]
    th = ((2 * kap.reshape(1, nk) + 1) * rho).astype(F32) * np.float32(2.0 * math.pi / (2 * n))
    lanes = lambda a: jnp.broadcast_to(a[:, :, None], (3, nk, LANES))
    return dict(f4=f4, g4=f4.T, twc=lanes(jnp.cos(th)), tws=lanes(jnp.sin(th)), nt=nt, nk=nk)


HY_RADIX = 4


HY_PRE_CHUNK = 512


def _hy_pre_kernel(x0_ref, x1_ref, v_ref, w_ref, b_ref, u_ref, x0c_ref, p0_ref, p1_ref, p2_ref, tu_ref, tx_ref):
    seq_len = x0_ref.shape[1]
    pads = (p0_ref, p1_ref, p2_ref)
    for p_ref, z_ref in zip(pads, (x0_ref, x1_ref, v_ref)):
        p_ref[0:SUBLANES] = jnp.zeros((SUBLANES, LANES), F32)
        p_ref[seq_len + SUBLANES:seq_len + 2 * SUBLANES] = jnp.zeros((SUBLANES, LANES), F32)
        p_ref[SUBLANES:seq_len + SUBLANES] = z_ref[0].astype(F32)

    for c in range(seq_len // HY_PRE_CHUNK):
        lo = SUBLANES + c * HY_PRE_CHUNK

        def conv(i):
            w = w_ref[i]
            p_ref = pads[i]
            return (w[0:1] * p_ref[lo - 1:lo - 1 + HY_PRE_CHUNK] + w[1:2] * p_ref[lo:lo + HY_PRE_CHUNK]
                    + w[2:3] * p_ref[lo + 1:lo + 1 + HY_PRE_CHUNK] + b_ref[i])

        rows = slice(c * HY_PRE_CHUNK, (c + 1) * HY_PRE_CHUNK)
        tx_ref[rows] = conv(0)
        tu_ref[rows] = conv(2) * conv(1)

    for o_ref, t_ref in ((x0c_ref, tx_ref), (u_ref, tu_ref)):
        for r in range(HY_RADIX):
            o_ref[0, :, r * LANES:(r + 1) * LANES] = t_ref[pl.ds(r, seq_len // HY_RADIX, stride=HY_RADIX), :].astype(o_ref.dtype)


def _hyena_pre(p3, lane0, conv_w, conv_b):
    bsz, seq_len, _ = p3.shape
    width = conv_w.shape[1] // 3
    nt = width // LANES
    off = lane0 // LANES
    w3 = conv_w.astype(F32).reshape(conv_w.shape[0], 3, width).transpose(1, 0, 2)
    b3 = conv_b.astype(F32).reshape(3, 1, width)
    blk = (1, seq_len, LANES)
    oblk = (1, seq_len // HY_RADIX, HY_RADIX * LANES)
    out = jax.ShapeDtypeStruct((bsz, seq_len // HY_RADIX, HY_RADIX * width), BF16)
    return pl.pallas_call(
        _hy_pre_kernel,
        grid=(bsz, nt),
        in_specs=[pl.BlockSpec(blk, lambda b, j: (b, 0, off + j)),
                  pl.BlockSpec(blk, lambda b, j: (b, 0, off + nt + j)),
                  pl.BlockSpec(blk, lambda b, j: (b, 0, off + 2 * nt + j)),
                  pl.BlockSpec((3, conv_w.shape[0], LANES), lambda b, j: (0, 0, j)),
                  pl.BlockSpec((3, 1, LANES), lambda b, j: (0, 0, j))],
        out_specs=[pl.BlockSpec(oblk, lambda b, j: (b, 0, j)),
                   pl.BlockSpec(oblk, lambda b, j: (b, 0, j))],
        out_shape=[out, out],
        scratch_shapes=[pltpu.VMEM((seq_len + 2 * SUBLANES, LANES), F32)] * 3 + [pltpu.VMEM((seq_len, LANES), F32)] * 2,
        compiler_params=_params("parallel", "parallel"),
        name="hyena_pre",
    )(p3, p3, p3, w3, b3)


def _lane_tile(a, width):
    return jnp.concatenate([a] * (width // a.shape[1]), axis=1)


def _rho_slice(a, r, width, chunk):
    return jnp.concatenate([a[:, (g * HY_RADIX + r) * chunk:(g * HY_RADIX + r + 1) * chunk]
                            for g in range(width // chunk)], axis=1)


def _hy_bins(q, twc_ref, tws_ref, width, chunk):
    tk = q.shape[0] // 2
    qr, qi = q[:tk], q[tk:]
    zr, zi = [_rho_slice(qr, 0, width, chunk)], [_rho_slice(qi, 0, width, chunk)]
    for r in range(1, HY_RADIX):
        c = _lane_tile(twc_ref[r - 1], width)
        s = _lane_tile(tws_ref[r - 1], width)
        a, b = _rho_slice(qr, r, width, chunk), _rho_slice(qi, r, width, chunk)
        zr.append(a * c + b * s)
        zi.append(b * c - a * s)
    t0r, t0i = zr[0] + zr[2], zi[0] + zi[2]
    t1r, t1i = zr[0] - zr[2], zi[0] - zi[2]
    t2r, t2i = zr[1] + zr[3], zi[1] + zi[3]
    t3r, t3i = zr[1] - zr[3], zi[1] - zi[3]
    return ((t0r + t2r, t0i + t2i), (t1r + t3i, t1i - t3r), (t1r - t3i, -t1i - t3r), (t0r - t2r, t2i - t0i))


def _hy_filter_kernel2(f_ref, h_ref, twc_ref, tws_ref, ss_ref, o_ref):
    width = o_ref.shape[2]
    q = jnp.dot(f_ref[...], h_ref[...], preferred_element_type=F32)
    ss = ss_ref[...]
    scale = lax.rsqrt(ss[:, :width] + ss[:, width:])
    for t, (xr, xi) in enumerate(_hy_bins(q, twc_ref, tws_ref, 2 * width, 2 * width)):
        o_ref[2 * t] = (xr[:, :width] + xr[:, width:]) * scale
        o_ref[2 * t + 1] = (xi[:, :width] - xi[:, width:]) * scale


def _hy_conv_kernel(f_ref, g_ref, u_ref, h_ref, twc_ref, tws_ref, x0_ref, skip_ref, o_ref, acc_ref, tmp_ref, *, scale):
    kt = pl.program_id(1)
    width = h_ref.shape[2]
    q = jnp.dot(f_ref[...], u_ref[0], preferred_element_type=F32)
    prod = []
    for t, (xr, xi) in enumerate(_hy_bins(q, twc_ref, tws_ref, width, LANES)):
        hr, hi = h_ref[2 * t], h_ref[2 * t + 1]
        prod.append((xr * hr - xi * hi, xr * hi + xi * hr))
    (yar, yai), (ybr, ybi), (ycr, yci), (ydr, ydi) = prod
    e0r, e0i = yar + ydr, yai - ydi
    e1r, e1i = yar - ydr, yai + ydi
    e2r, e2i = ybr + ycr, ybi - yci
    e3r, e3i = ybr - ycr, ybi + yci
    sr = [e0r + e2r, e1r - e3i, e0r - e2r, e1r + e3i]
    si = [e0i + e2i, e1i + e3r, e0i - e2i, e1i - e3r]
    vr, vi = [sr[0]], [si[0]]
    for r in range(1, HY_RADIX):
        c = _lane_tile(twc_ref[r - 1], width)
        s = _lane_tile(tws_ref[r - 1], width)
        vr.append(sr[r] * c - si[r] * s)
        vi.append(sr[r] * s + si[r] * c)
    merged = lambda parts: jnp.concatenate([parts[r][:, g * LANES:(g + 1) * LANES]
                                            for g in range(width // LANES) for r in range(HY_RADIX)], axis=1)
    v = jnp.concatenate([merged(vr), merged(vi)], axis=0).astype(BF16)
    part = jnp.dot(g_ref[...], v, preferred_element_type=F32)

    @pl.when(kt == 0)
    def _():
        acc_ref[...] = part

    @pl.when(kt > 0)
    def _():
        acc_ref[...] += part

    @pl.when(kt == pl.num_programs(1) - 1)
    def _():
        rows = acc_ref.shape[0]
        for g in range(width // LANES):
            for r in range(HY_RADIX):
                lanes = slice((g * HY_RADIX + r) * LANES, (g * HY_RADIX + r + 1) * LANES)
                y = (acc_ref[:, lanes] * np.float32(scale)
                     + u_ref[0, :, lanes].astype(F32) * skip_ref[:, g * LANES:(g + 1) * LANES])
                tmp_ref[pl.ds(r, rows, stride=HY_RADIX), :] = y * x0_ref[0, :, lanes].astype(F32)
            o_ref[0, :, g * LANES:(g + 1) * LANES] = tmp_ref[...].astype(o_ref.dtype)


def _hyena_long_conv(u, x0c, h, ss, skip):
    bsz, nt_u, width4 = u.shape
    seq_len, width = nt_u * HY_RADIX, width4 // HY_RADIX
    tb = _hy_tables(seq_len)
    nt, nk = tb["nt"], tb["nk"]
    nkt = nk // HY_TK
    tw_spec = lambda im: pl.BlockSpec((3, HY_TK, LANES), im)
    hhat = pl.pallas_call(
        _hy_filter_kernel2,
        grid=(nkt,),
        in_specs=[pl.BlockSpec((2 * HY_TK, nt), lambda i: (i, 0)),
                  _resident((nt, 8 * width), lambda i: (0, 0)),
                  tw_spec(lambda i: (0, i, 0)), tw_spec(lambda i: (0, i, 0)),
                  _resident((1, 2 * width), lambda i: (0, 0))],
        out_specs=pl.BlockSpec((8, HY_TK, width), lambda i: (0, i, 0)),
        out_shape=jax.ShapeDtypeStruct((8, nk, width), F32),
        compiler_params=_params("parallel"),
        name="hyena_filter_spectrum",
    )(tb["f4"], h.reshape(nt, 8 * width), tb["twc"], tb["tws"], ss)
    row4 = (1, nt, 4 * width)
    return pl.pallas_call(
        functools.partial(_hy_conv_kernel, scale=1.0 / seq_len),
        grid=(bsz, nkt),
        in_specs=[pl.BlockSpec((2 * HY_TK, nt), lambda b, i: (i, 0)),
                  pl.BlockSpec((nt, 2 * HY_TK), lambda b, i: (0, i)),
                  pl.BlockSpec(row4, lambda b, i: (b, 0, 0)),
                  pl.BlockSpec((8, HY_TK, width), lambda b, i: (0, i, 0)),
                  tw_spec(lambda b, i: (0, i, 0)), tw_spec(lambda b, i: (0, i, 0)),
                  pl.BlockSpec(row4, lambda b, i: (b, 0, 0), pipeline_mode=pl.Buffered(1)),
                  _resident((1, width), lambda b, i: (0, 0))],
        out_specs=pl.BlockSpec((1, seq_len, width), lambda b, i: (b, 0, 0)),
        out_shape=jax.ShapeDtypeStruct((bsz, seq_len, width), BF16),
        scratch_shapes=[pltpu.VMEM((nt, 4 * width), F32), pltpu.VMEM((seq_len, LANES), F32)],
        compiler_params=_params("parallel", "arbitrary"),
        name="hyena_long_conv",
    )(tb["f4"], tb["g4"], u, hhat, tb["twc"], tb["tws"], x0c, skip.astype(F32).reshape(1, width))


def _sgu_kernel(zu_ref, zv_ref, g_ref, b_ref, w_ref, sb_ref, o_ref):
    tokens, width = zu_ref.shape[1], zu_ref.shape[2]
    u = _gelu(zu_ref[0].astype(F32))
    v = _layer_norm(_gelu(zv_ref[0].astype(F32)), g_ref[...], b_ref[...]).astype(BF16)
    lane = lax.broadcasted_iota(jnp.int32, (SG_CHUNK, LANES), 1)
    first = lane < LANES // 2
    for c in range(tokens // SG_CHUNK):
        rows = slice(c * SG_CHUNK, (c + 1) * SG_CHUNK)
        for p in range(width // LANES):
            cols = slice(p * LANES, (p + 1) * LANES)
            r = jnp.dot(w_ref[p], v[rows, cols], preferred_element_type=F32)
            sv = jnp.where(first, r[:SG_CHUNK], r[SG_CHUNK:]) + sb_ref[:, cols]
            o_ref[0, rows, cols] = (u[rows, cols] * sv).astype(o_ref.dtype)


def _spatial_gating(p3, ln_g, ln_b, sg_w, sg_b, tokens=512):
    bsz, seq_len, _ = p3.shape
    groups = sg_w.shape[0]
    width = ln_g.shape[0]
    gw = width // groups
    assert 2 * gw == LANES and sg_w.shape[1] == SG_CHUNK
    wst = sg_w.astype(BF16).reshape(groups // 2, 2 * SG_CHUNK, SG_CHUNK)
    sb = jnp.repeat(sg_b.astype(F32).T, gw, axis=1)
    blk = (1, tokens, width)
    return pl.pallas_call(
        _sgu_kernel,
        grid=(bsz, seq_len // tokens),
        in_specs=[pl.BlockSpec(blk, lambda b, i: (b, i, 0)),
                  pl.BlockSpec(blk, lambda b, i: (b, i, 1)),
                  _resident((1, width), lambda b, i: (0, 0)),
                  _resident((1, width), lambda b, i: (0, 0)),
                  _resident((groups // 2, 2 * SG_CHUNK, SG_CHUNK), lambda b, i: (0, 0, 0)),
                  _resident((SG_CHUNK, width), lambda b, i: (0, 0))],
        out_specs=pl.BlockSpec(blk, lambda b, i: (b, i, 0)),
        out_shape=jax.ShapeDtypeStruct((bsz, seq_len, width), BF16),
        compiler_params=_params("parallel", "parallel"),
        name="spatial_gating",
    )(p3, p3, ln_g.astype(F32).reshape(1, width), ln_b.astype(F32).reshape(1, width), wst, sb)


CV_PAD = 16
CV_ROWS = 64
CV_UNROLL = 4


def _conf_kernel(za_ref, zg_ref, w_ref, wb_ref, g_ref, b_ref, o_ref, apad_ref):
    seq_len, width = za_ref.shape[1], za_ref.shape[2]
    fill_rows = 256
    pad_tiles = CV_PAD // SUBLANES
    out_tiles = CV_ROWS // SUBLANES
    apad_ref[0:pad_tiles] = jnp.zeros((pad_tiles, SUBLANES, width), F32)
    apad_ref[seq_len // SUBLANES + pad_tiles:seq_len // SUBLANES + 2 * pad_tiles] = jnp.zeros(
        (pad_tiles, SUBLANES, width), F32)

    def fill(i, carry):
        r0 = pl.multiple_of(i * fill_rows, fill_rows)
        a = za_ref[0, pl.ds(r0, fill_rows), :].astype(F32)
        g = zg_ref[0, pl.ds(r0, fill_rows), :].astype(F32)
        apad_ref[pl.ds(i * (fill_rows // SUBLANES) + pad_tiles, fill_rows // SUBLANES)] = (
            (a * jax.nn.sigmoid(g)).reshape(fill_rows // SUBLANES, SUBLANES, width))
        return carry

    lax.fori_loop(0, seq_len // fill_rows, fill, 0)
    sub = lax.broadcasted_iota(jnp.int32, (out_tiles, SUBLANES, width), 1)

    def conv(i, carry):
        r0 = pl.multiple_of(i * CV_ROWS, CV_ROWS)
        off0 = CV_PAD - CV_KERNEL // 2
        acc = None
        for s in range(SUBLANES):
            part = None
            for q in range((off0 + CV_KERNEL - 1) // SUBLANES + 1):
                j = SUBLANES * q + s - off0
                if 0 <= j < CV_KERNEL:
                    term = w_ref[j][None] * apad_ref[pl.ds(i * out_tiles + q, out_tiles + 1)]
                    part = term if part is None else part + term
            if s:
                rot = pltpu.roll(part, SUBLANES - s, 1)
                part = jnp.where(sub < SUBLANES - s, rot[:out_tiles], rot[1:])
            else:
                part = part[:out_tiles]
            acc = part if acc is None else acc + part
        y = _layer_norm(acc + wb_ref[...][None], g_ref[...][None], b_ref[...][None])
        y = y * jax.nn.sigmoid(y)
        o_ref[0, pl.ds(r0, CV_ROWS), :] = y.reshape(CV_ROWS, width).astype(o_ref.dtype)
        return carry

    lax.fori_loop(0, seq_len // CV_ROWS, conv, 0, unroll=CV_UNROLL)


def _conformer_conv(p3, lane_block, dw_w, dw_b, ln_g, ln_b):
    bsz, seq_len, _ = p3.shape
    width = dw_w.shape[1]
    blk = (1, seq_len, width)
    vec = lambda a: a.astype(F32).reshape(1, width)
    w8 = jnp.broadcast_to(dw_w.astype(F32)[:, None, :], (CV_KERNEL, SUBLANES, width))
    return pl.pallas_call(
        _conf_kernel,
        grid=(bsz,),
        in_specs=[pl.BlockSpec(blk, lambda b: (b, 0, lane_block)),
                  pl.BlockSpec(blk, lambda b: (b, 0, lane_block + 1)),
                  _resident((CV_KERNEL, SUBLANES, width), lambda b: (0, 0, 0)),
                  _resident((1, width), lambda b: (0, 0)),
                  _resident((1, width), lambda b: (0, 0)),
                  _resident((1, width), lambda b: (0, 0))],
        out_specs=pl.BlockSpec(blk, lambda b: (b, 0, 0)),
        out_shape=jax.ShapeDtypeStruct((bsz, seq_len, width), BF16),
        scratch_shapes=[pltpu.VMEM(((seq_len + 2 * CV_PAD) // SUBLANES, SUBLANES, width), F32)],
        compiler_params=_params("parallel"),
        name="conformer_conv",
    )(p3, p3, w8, vec(dw_b), vec(ln_g), vec(ln_b))


def _out_mlp_kernel(x_ref, a_ref, b_ref, woa_ref, wob_ref, g_ref, w1_ref, w2_ref, gf_ref, o_ref,
                    *, hidden_chunk, final):
    x1 = (x_ref[...]
          + jnp.dot(a_ref[...], woa_ref[...], preferred_element_type=F32)
          + jnp.dot(b_ref[...], wob_ref[...], preferred_element_type=F32))
    hn = _rms(x1, g_ref[...]).astype(BF16)
    mlp = None
    for c in range(w1_ref.shape[1] // hidden_chunk):
        cols = slice(c * hidden_chunk, (c + 1) * hidden_chunk)
        t = jnp.maximum(jnp.dot(hn, w1_ref[:, cols], preferred_element_type=F32), 0.0)
        y = jnp.dot((t * t).astype(BF16), w2_ref[cols, :], preferred_element_type=F32)
        mlp = y if mlp is None else mlp + y
    acc = x1 + mlp
    if final:
        acc = _rms(acc, gf_ref[...])
    o_ref[...] = acc


def _out_mlp(x2, a, b, w_out, g, w1, w2, gf, final, tm=512, hidden_chunk=1024):
    m, d = x2.shape
    half = a.shape[1]
    hidden = w1.shape[1]
    return pl.pallas_call(
        functools.partial(_out_mlp_kernel, hidden_chunk=hidden_chunk, final=final),
        grid=(m // tm,),
        in_specs=[pl.BlockSpec((tm, d), lambda i: (i, 0)),
                  pl.BlockSpec((tm, half), lambda i: (i, 0)),
                  pl.BlockSpec((tm, half), lambda i: (i, 0)),
                  _resident((half, d), lambda i: (0, 0)),
                  _resident((half, d), lambda i: (1, 0)),
                  _resident((1, d), lambda i: (0, 0)),
                  _resident((d, hidden), lambda i: (0, 0)),
                  _resident((hidden, d), lambda i: (0, 0)),
                  _resident((1, d), lambda i: (0, 0))],
        out_specs=pl.BlockSpec((tm, d), lambda i: (i, 0)),
        out_shape=jax.ShapeDtypeStruct((m, d), F32),
        compiler_params=_params("parallel"),
        name="out_mlp",
    )(x2, a, b, w_out, w_out, g.reshape(1, d), w1, w2, gf.reshape(1, d))


def kernel(x, norm_g, ev_w_in, ev_rpb, hy_conv_w, hy_conv_b, hy_w1, hy_b1, hy_w2, hy_b2, hy_w3, hy_b3, hy_freq, hy_w_out, hy_skip, ev_w_out, od_w_in, sg_ln_g, sg_ln_b, sg_w, sg_b, cv_dw_w, cv_dw_b, cv_ln_g, cv_ln_b, od_w_out, mlp_w1, mlp_w2, final_g):
    bsz, seq_len, d = x.shape
    depth = norm_g.shape[0]
    m = bsz * seq_len
    x2 = x.astype(F32).reshape(m, d)
    na_width = ev_rpb.shape[1] * NA_HEAD_DIM
    for i in range(depth):
        j = i // 2
        if i % 2 == 0:
            p3 = _norm_proj(x2, norm_g[i, 0], ev_w_in[j].astype(BF16)).reshape(bsz, seq_len, -1)
            mix_a = _neighbourhood_attention(p3, ev_rpb[j])
            u, x0c = _hyena_pre(p3, 3 * na_width, hy_conv_w[j], hy_conv_b[j])
            h, ss = _hyena_filters(seq_len, hy_w1[j], hy_b1[j], hy_w2[j], hy_b2[j], hy_w3[j], hy_b3[j],
                                   hy_freq[j], hy_w_out[j])
            mix_b = _hyena_long_conv(u, x0c, h, ss, hy_skip[j])
            w_out = ev_w_out[j]
        else:
            p3 = _norm_proj(x2, norm_g[i, 0], od_w_in[j].astype(BF16)).reshape(bsz, seq_len, -1)
            mix_a = _spatial_gating(p3, sg_ln_g[j], sg_ln_b[j], sg_w[j], sg_b[j])
            mix_b = _conformer_conv(p3, 2, cv_dw_w[j], cv_dw_b[j], cv_ln_g[j], cv_ln_b[j])
            w_out = od_w_out[j]
        x2 = _out_mlp(x2, mix_a.reshape(m, -1), mix_b.reshape(m, -1), w_out.astype(BF16), norm_g[i, 1],
                      mlp_w1[i].astype(BF16), mlp_w2[i].astype(BF16), final_g, final=(i == depth - 1))
    return x2.reshape(bsz, seq_len, d).astype(x.dtype)
```

```python
import functools
import math

import numpy as np
import jax
import jax.numpy as jnp
from jax import lax
from jax.experimental import pallas as pl
from jax.experimental.pallas import tpu as pltpu

F32 = jnp.float32
BF16 = jnp.bfloat16
EPS = 1e-6
MASK_VALUE = -1e30

GRID_W = 64
NA_HEAD_DIM = 64
NA_WIN_R = 8
NA_WIN_C = 16
HY_EMB = 33
HY_TARGET = 1e-2
HY_FAST_PCT = 0.3
HY_SLOW_PCT = 1.5
SG_CHUNK = 128
CV_KERNEL = 31

LANES = 128
SUBLANES = 8
VMEM_LIMIT = 56 * 1024 * 1024


def _params(*sem):
    return pltpu.CompilerParams(dimension_semantics=sem, vmem_limit_bytes=VMEM_LIMIT)


def _resident(shape, index_map):
    return pl.BlockSpec(shape, index_map, pipeline_mode=pl.Buffered(1))


def _rms(x, g):
    return x * lax.rsqrt(jnp.mean(x * x, axis=-1, keepdims=True) + EPS) * g


def _layer_norm(x, g, b):
    xc = x - jnp.mean(x, axis=-1, keepdims=True)
    return xc * lax.rsqrt(jnp.mean(xc * xc, axis=-1, keepdims=True) + EPS) * g + b


def _gelu(x):
    return x * (lax.erf(x / np.float32(math.sqrt(2.0))) + 1.0) / 2.0


def _norm_proj_kernel(x_ref, g_ref, w_ref, o_ref):
    hn = _rms(x_ref[...], g_ref[...]).astype(BF16)
    o_ref[...] = jnp.dot(hn, w_ref[...], preferred_element_type=F32).astype(o_ref.dtype)


def _norm_proj(x2, g, w, tm=1024):
    m, d = x2.shape
    n = w.shape[1]
    return pl.pallas_call(
        _norm_proj_kernel,
        grid=(m // tm,),
        in_specs=[pl.BlockSpec((tm, d), lambda i: (i, 0)),
                  _resident((1, d), lambda i: (0, 0)),
                  _resident((d, n), lambda i: (0, 0))],
        out_specs=pl.BlockSpec((tm, n), lambda i: (i, 0)),
        out_shape=jax.ShapeDtypeStruct((m, n), BF16),
        compiler_params=_params("parallel"),
        name="norm_proj",
    )(x2, g.reshape(1, d), w)


NA_UNROLL = 8


def _na_kernel(q_ref, k_ref, v_ref, tb_ref, o_ref):
    seq_len = q_ref.shape[1]
    rows = seq_len // GRID_W
    nkeys = NA_WIN_R * GRID_W
    lane = lax.broadcasted_iota(jnp.int32, (GRID_W, LANES), 1)
    first = lane < NA_HEAD_DIM

    def body(i, carry):
        q0s, k0s, cases, scores = [], [], [], []
        for u in range(NA_UNROLL):
            r = i * NA_UNROLL + u
            rs = jnp.clip(r - NA_WIN_R // 2, 0, rows - NA_WIN_R)
            cases.append(r - rs)
            q0s.append(pl.multiple_of(r * GRID_W, GRID_W))
            k0s.append(pl.multiple_of(rs * GRID_W, GRID_W))
            q2 = q_ref[0, pl.ds(q0s[u], GRID_W), :] * BF16(NA_HEAD_DIM ** -0.5)
            k2 = k_ref[0, pl.ds(k0s[u], nkeys), :]
            zero = jnp.zeros_like(q2)
            qs = jnp.concatenate([jnp.where(first, q2, zero), jnp.where(first, zero, q2)], axis=0)
            scores.append(lax.dot_general(qs, k2, (((1,), (1,)), ((), ())), preferred_element_type=F32))
        probs, dens = [], []
        for u in range(NA_UNROLL):
            s = scores[u] + tb_ref[0, cases[u]]
            e = jnp.exp(s - jnp.max(s, axis=-1, keepdims=True))
            dens.append(jnp.sum(e, axis=-1, keepdims=True))
            probs.append(e.astype(BF16))
        outs = [jnp.dot(probs[u], v_ref[0, pl.ds(k0s[u], nkeys), :], preferred_element_type=F32)
                for u in range(NA_UNROLL)]
        for u in range(NA_UNROLL):
            o = outs[u] / dens[u]
            o_ref[0, pl.ds(q0s[u], GRID_W), :] = jnp.where(first, o[:GRID_W], o[GRID_W:]).astype(o_ref.dtype)
        return carry

    lax.fori_loop(0, rows // NA_UNROLL, body, 0)


def _na_bias_table(rpb):
    heads, nrow, ncol = rpb.shape
    qc = np.arange(GRID_W)[:, None]
    kc = np.arange(GRID_W)[None, :]
    cs = np.clip(qc - NA_WIN_C // 2, 0, GRID_W - NA_WIN_C)
    inside = (kc >= cs) & (kc < cs + NA_WIN_C)
    col_off = kc - qc + (NA_WIN_C - 1)
    onehot = (col_off[None] == np.arange(ncol)[:, None, None]) & inside[None]
    expand = jnp.asarray(onehot.reshape(ncol, GRID_W * GRID_W), F32)
    t = jnp.dot(rpb.astype(F32).reshape(heads * nrow, ncol), expand, precision=lax.Precision.HIGHEST)
    t = jnp.where(inside[None, None], t.reshape(heads, nrow, GRID_W, GRID_W), MASK_VALUE)
    t = jnp.stack([t[:, NA_WIN_R - 1 - c:2 * NA_WIN_R - 1 - c] for c in range(NA_WIN_R)], axis=1)
    t = t.transpose(0, 1, 3, 2, 4).reshape(heads // 2, 2, NA_WIN_R, GRID_W, NA_WIN_R * GRID_W)
    return t.transpose(0, 2, 1, 3, 4).reshape(heads // 2, NA_WIN_R, 2 * GRID_W, NA_WIN_R * GRID_W)


def _neighbourhood_attention(p3, rpb):
    bsz, seq_len, _ = p3.shape
    heads = rpb.shape[0]
    pairs = heads // 2
    tb = _na_bias_table(rpb)
    blk = (1, seq_len, LANES)
    return pl.pallas_call(
        _na_kernel,
        grid=(bsz, pairs),
        in_specs=[pl.BlockSpec(blk, lambda b, j: (b, 0, j)),
                  pl.BlockSpec(blk, lambda b, j: (b, 0, pairs + j)),
                  pl.BlockSpec(blk, lambda b, j: (b, 0, 2 * pairs + j)),
                  pl.BlockSpec((1, NA_WIN_R, 2 * GRID_W, NA_WIN_R * GRID_W), lambda b, j: (j, 0, 0, 0))],
        out_specs=pl.BlockSpec(blk, lambda b, j: (b, 0, j)),
        out_shape=jax.ShapeDtypeStruct((bsz, seq_len, pairs * LANES), BF16),
        compiler_params=_params("parallel", "parallel"),
        name="na_attention",
    )(p3, p3, p3, tb)


def _hy_filter_kernel(frq_ref, w1_ref, b1_ref, w2_ref, b2_ref, w3_ref, b3_ref, fq_ref, fo_ref,
                      dl_ref, h_ref, ss_ref, *, seq_len):
    i = pl.program_id(0)
    tl = h_ref.shape[0]
    hp = lax.Precision.HIGHEST
    pos = (lax.broadcasted_iota(jnp.int32, (tl, LANES), 0) + i * tl).astype(F32)
    lane = lax.broadcasted_iota(jnp.int32, (tl, LANES), 1)
    bands = (HY_EMB - 1) // 2
    t = pos / np.float32(seq_len - 1)
    ang = (np.float32(2.0 * math.pi) * pos / np.float32(seq_len)) * frq_ref[...]
    z = jnp.where(lane == 0, t,
                  jnp.where(lane <= bands, jnp.cos(ang),
                            jnp.where(lane <= 2 * bands, -jnp.sin(ang), 0.0)))
    fq = fq_ref[...]
    hid = jnp.sin(fq * (jnp.dot(z, w1_ref[...], precision=hp, preferred_element_type=F32) + b1_ref[...]))
    hid = jnp.sin(fq * (jnp.dot(hid, w2_ref[...], precision=hp, preferred_element_type=F32) + b2_ref[...]))
    hid = jnp.sin(fq * (jnp.dot(hid, w3_ref[...], precision=hp, preferred_element_type=F32) + b3_ref[...]))
    h = jnp.dot(hid, fo_ref[...], precision=hp, preferred_element_type=F32)
    h = h * jnp.exp(-t[:, :1] * dl_ref[...])
    h_ref[...] = h.astype(h_ref.dtype)

    @pl.when(i == 0)
    def _():
        ss_ref[...] = jnp.zeros_like(ss_ref)

    ss_ref[...] += jnp.sum(h * h, axis=0, keepdims=True)


def _pad2(a, rows, cols):
    return jnp.pad(a.astype(F32), ((0, rows - a.shape[0]), (0, cols - a.shape[1])))


def _hyena_filters(seq_len, w1, b1, w2, b2, w3, b3, freq, f_out, tl=512):
    ffn = w1.shape[1]
    width2 = f_out.shape[1]
    bands = (HY_EMB - 1) // 2
    fr = np.linspace(1e-4, bands - 1, bands, dtype=np.float32)
    frq = np.zeros((1, LANES), np.float32)
    frq[0, 1:1 + bands] = fr
    frq[0, 1 + bands:1 + 2 * bands] = fr
    deltas = np.abs(np.linspace(math.log(HY_TARGET) / HY_SLOW_PCT, math.log(HY_TARGET) / HY_FAST_PCT,
                                width2 // 2, dtype=np.float32))
    dl = np.tile(deltas, 2)[None, :]
    row = lambda a: _pad2(a.reshape(1, -1), 1, LANES)
    small = lambda shape: _resident(shape, lambda i: (0, 0))
    return pl.pallas_call(
        functools.partial(_hy_filter_kernel, seq_len=seq_len),
        grid=(seq_len // tl,),
        in_specs=[small((1, LANES)), small((LANES, LANES)), small((1, LANES)), small((LANES, LANES)),
                  small((1, LANES)), small((LANES, LANES)), small((1, LANES)), small((1, LANES)),
                  small((LANES, width2)), small((1, width2))],
        out_specs=[pl.BlockSpec((tl, width2), lambda i: (i, 0)),
                   pl.BlockSpec((1, width2), lambda i: (0, 0))],
        out_shape=[jax.ShapeDtypeStruct((seq_len, width2), BF16),
                   jax.ShapeDtypeStruct((1, width2), F32)],
        compiler_params=_params("arbitrary"),
        name="hyena_filters",
    )(jnp.asarray(frq), _pad2(w1, LANES, LANES), row(b1), _pad2(w2, LANES, LANES), row(b2),
      _pad2(w3, LANES, LANES), row(b3), row(freq), _pad2(f_out, LANES, width2), jnp.asarray(dl))


HY_TK = 256


def _hy_tables(seq_len):
    n = 2 * seq_len
    nt, nk = seq_len // 4, n // 8
    kap = jnp.arange(nk, dtype=jnp.int32)[:, None]
    tau = jnp.arange(nt, dtype=jnp.int32)[None, :]
    phase = ((2 * kap + 1) * tau) % (n // 2)
    ang = phase.astype(F32) * np.float32(2.0 * math.pi / (n // 2))
    c = jnp.cos(ang).reshape(nk // HY_TK, 1, HY_TK, nt)
    s = (-jnp.sin(ang)).reshape(nk // HY_TK, 1, HY_TK, nt)
    f4 = jnp.concatenate([c, s], axis=1).reshape(2 * nk, nt).astype(BF16)
    rho = jnp.arange(1, 4, dtype=jnp.int32)[:, None]
    th = ((2 * kap.reshape(1, nk) + 1) * rho).astype(F32) * np.float32(2.0 * math.pi / (2 * n))
    lanes = lambda a: jnp.broadcast_to(a[:, :, None], (3, nk, LANES))
    return dict(f4=f4, g4=f4.T, twc=lanes(jnp.cos(th)), tws=lanes(jnp.sin(th)), nt=nt, nk=nk)


HY_RADIX = 4


HY_PRE_CHUNK = 512


def _hy_pre_kernel(x0_ref, x1_ref, v_ref, w_ref, b_ref, u_ref, x0c_ref, p0_ref, p1_ref, p2_ref, tu_ref, tx_ref):
    seq_len = x0_ref.shape[1]
    pads = (p0_ref, p1_ref, p2_ref)
    for p_ref, z_ref in zip(pads, (x0_ref, x1_ref, v_ref)):
        p_ref[0:SUBLANES] = jnp.zeros((SUBLANES, LANES), F32)
        p_ref[seq_len + SUBLANES:seq_len + 2 * SUBLANES] = jnp.zeros((SUBLANES, LANES), F32)
        p_ref[SUBLANES:seq_len + SUBLANES] = z_ref[0].astype(F32)

    for c in range(seq_len // HY_PRE_CHUNK):
        lo = SUBLANES + c * HY_PRE_CHUNK

        def conv(i):
            w = w_ref[i]
            p_ref = pads[i]
            return (w[0:1] * p_ref[lo - 1:lo - 1 + HY_PRE_CHUNK] + w[1:2] * p_ref[lo:lo + HY_PRE_CHUNK]
                    + w[2:3] * p_ref[lo + 1:lo + 1 + HY_PRE_CHUNK] + b_ref[i])

        rows = slice(c * HY_PRE_CHUNK, (c + 1) * HY_PRE_CHUNK)
        tx_ref[rows] = conv(0)
        tu_ref[rows] = conv(2) * conv(1)

    for o_ref, t_ref in ((x0c_ref, tx_ref), (u_ref, tu_ref)):
        for r in range(HY_RADIX):
            o_ref[0, :, r * LANES:(r + 1) * LANES] = t_ref[pl.ds(r, seq_len // HY_RADIX, stride=HY_RADIX), :].astype(o_ref.dtype)


def _hyena_pre(p3, lane0, conv_w, conv_b):
    bsz, seq_len, _ = p3.shape
    width = conv_w.shape[1] // 3
    nt = width // LANES
    off = lane0 // LANES
    w3 = conv_w.astype(F32).reshape(conv_w.shape[0], 3, width).transpose(1, 0, 2)
    b3 = conv_b.astype(F32).reshape(3, 1, width)
    blk = (1, seq_len, LANES)
    oblk = (1, seq_len // HY_RADIX, HY_RADIX * LANES)
    out = jax.ShapeDtypeStruct((bsz, seq_len // HY_RADIX, HY_RADIX * width), BF16)
    return pl.pallas_call(
        _hy_pre_kernel,
        grid=(bsz, nt),
        in_specs=[pl.BlockSpec(blk, lambda b, j: (b, 0, off + j)),
                  pl.BlockSpec(blk, lambda b, j: (b, 0, off + nt + j)),
                  pl.BlockSpec(blk, lambda b, j: (b, 0, off + 2 * nt + j)),
                  pl.BlockSpec((3, conv_w.shape[0], LANES), lambda b, j: (0, 0, j)),
                  pl.BlockSpec((3, 1, LANES), lambda b, j: (0, 0, j))],
        out_specs=[pl.BlockSpec(oblk, lambda b, j: (b, 0, j)),
                   pl.BlockSpec(oblk, lambda b, j: (b, 0, j))],
        out_shape=[out, out],
        scratch_shapes=[pltpu.VMEM((seq_len + 2 * SUBLANES, LANES), F32)] * 3 + [pltpu.VMEM((seq_len, LANES), F32)] * 2,
        compiler_params=_params("parallel", "parallel"),
        name="hyena_pre",
    )(p3, p3, p3, w3, b3)


def _lane_tile(a, width):
    return jnp.concatenate([a] * (width // a.shape[1]), axis=1)


def _rho_slice(a, r, width, chunk):
    return jnp.concatenate([a[:, (g * HY_RADIX + r) * chunk:(g * HY_RADIX + r + 1) * chunk]
                            for g in range(width // chunk)], axis=1)


def _hy_bins(q, twc_ref, tws_ref, width, chunk):
    tk = q.shape[0] // 2
    qr, qi = q[:tk], q[tk:]
    zr, zi = [_rho_slice(qr, 0, width, chunk)], [_rho_slice(qi, 0, width, chunk)]
    for r in range(1, HY_RADIX):
        c = _lane_tile(twc_ref[r - 1], width)
        s = _lane_tile(tws_ref[r - 1], width)
        a, b = _rho_slice(qr, r, width, chunk), _rho_slice(qi, r, width, chunk)
        zr.append(a * c + b * s)
        zi.append(b * c - a * s)
    t0r, t0i = zr[0] + zr[2], zi[0] + zi[2]
    t1r, t1i = zr[0] - zr[2], zi[0] - zi[2]
    t2r, t2i = zr[1] + zr[3], zi[1] + zi[3]
    t3r, t3i = zr[1] - zr[3], zi[1] - zi[3]
    return ((t0r + t2r, t0i + t2i), (t1r + t3i, t1i - t3r), (t1r - t3i, -t1i - t3r), (t0r - t2r, t2i - t0i))


def _hy_filter_kernel2(f_ref, h_ref, twc_ref, tws_ref, ss_ref, o_ref):
    width = o_ref.shape[2]
    q = jnp.dot(f_ref[...], h_ref[...], preferred_element_type=F32)
    ss = ss_ref[...]
    scale = lax.rsqrt(ss[:, :width] + ss[:, width:])
    for t, (xr, xi) in enumerate(_hy_bins(q, twc_ref, tws_ref, 2 * width, 2 * width)):
        o_ref[2 * t] = (xr[:, :width] + xr[:, width:]) * scale
        o_ref[2 * t + 1] = (xi[:, :width] - xi[:, width:]) * scale


def _hy_conv_kernel(f_ref, g_ref, u_ref, h_ref, twc_ref, tws_ref, x0_ref, skip_ref, o_ref, acc_ref, tmp_ref, *, scale):
    kt = pl.program_id(1)
    width = h_ref.shape[2]
    group = HY_RADIX * LANES

    @pl.when(kt == 0)
    def _():
        acc_ref[...] = jnp.zeros_like(acc_ref)

    tw = [(twc_ref[r - 1], tws_ref[r - 1]) for r in range(1, HY_RADIX)]
    for g in range(width // LANES):
        lanes = slice(g * group, (g + 1) * group)
        ch = slice(g * LANES, (g + 1) * LANES)
        q = jnp.dot(f_ref[...], u_ref[0, :, lanes], preferred_element_type=F32)
        prod = []
        for t, (xr, xi) in enumerate(_hy_bins(q, twc_ref, tws_ref, LANES, LANES)):
            hr, hi = h_ref[2 * t, :, ch], h_ref[2 * t + 1, :, ch]
            prod.append((xr * hr - xi * hi, xr * hi + xi * hr))
        (yar, yai), (ybr, ybi), (ycr, yci), (ydr, ydi) = prod
        e0r, e0i = yar + ydr, yai - ydi
        e1r, e1i = yar - ydr, yai + ydi
        e2r, e2i = ybr + ycr, ybi - yci
        e3r, e3i = ybr - ycr, ybi + yci
        sr = [e0r + e2r, e1r - e3i, e0r - e2r, e1r + e3i]
        si = [e0i + e2i, e1i + e3r, e0i - e2i, e1i - e3r]
        vr, vi = [sr[0]], [si[0]]
        for r in range(1, HY_RADIX):
            c, s = tw[r - 1]
            vr.append(sr[r] * c - si[r] * s)
            vi.append(sr[r] * s + si[r] * c)
        v = jnp.concatenate([jnp.concatenate(vr, axis=1), jnp.concatenate(vi, axis=1)], axis=0).astype(BF16)
        acc_ref[:, lanes] += jnp.dot(g_ref[...], v, preferred_element_type=F32)

    @pl.when(kt == pl.num_programs(1) - 1)
    def _():
        rows = acc_ref.shape[0]
        for g in range(width // LANES):
            for r in range(HY_RADIX):
                lanes = slice((g * HY_RADIX + r) * LANES, (g * HY_RADIX + r + 1) * LANES)
                y = (acc_ref[:, lanes] * np.float32(scale)
                     + u_ref[0, :, lanes].astype(F32) * skip_ref[:, g * LANES:(g + 1) * LANES])
                tmp_ref[pl.ds(r, rows, stride=HY_RADIX), :] = y * x0_ref[0, :, lanes].astype(F32)
            o_ref[0, :, g * LANES:(g + 1) * LANES] = tmp_ref[...].astype(o_ref.dtype)


def _hyena_long_conv(u, x0c, h, ss, skip):
    bsz, nt_u, width4 = u.shape
    seq_len, width = nt_u * HY_RADIX, width4 // HY_RADIX
    tb = _hy_tables(seq_len)
    nt, nk = tb["nt"], tb["nk"]
    nkt = nk // HY_TK
    tw_spec = lambda im: pl.BlockSpec((3, HY_TK, LANES), im)
    hhat = pl.pallas_call(
        _hy_filter_kernel2,
        grid=(nkt,),
        in_specs=[pl.BlockSpec((2 * HY_TK, nt), lambda i: (i, 0)),
                  _resident((nt, 8 * width), lambda i: (0, 0)),
                  tw_spec(lambda i: (0, i, 0)), tw_spec(lambda i: (0, i, 0)),
                  _resident((1, 2 * width), lambda i: (0, 0))],
        out_specs=pl.BlockSpec((8, HY_TK, width), lambda i: (0, i, 0)),
        out_shape=jax.ShapeDtypeStruct((8, nk, width), F32),
        compiler_params=_params("parallel"),
        name="hyena_filter_spectrum",
    )(tb["f4"], h.reshape(nt, 8 * width), tb["twc"], tb["tws"], ss)
    row4 = (1, nt, 4 * width)
    return pl.pallas_call(
        functools.partial(_hy_conv_kernel, scale=1.0 / seq_len),
        grid=(bsz, nkt),
        in_specs=[pl.BlockSpec((2 * HY_TK, nt), lambda b, i: (i, 0)),
                  pl.BlockSpec((nt, 2 * HY_TK), lambda b, i: (0, i)),
                  pl.BlockSpec(row4, lambda b, i: (b, 0, 0)),
                  pl.BlockSpec((8, HY_TK, width), lambda b, i: (0, i, 0)),
                  tw_spec(lambda b, i: (0, i, 0)), tw_spec(lambda b, i: (0, i, 0)),
                  pl.BlockSpec(row4, lambda b, i: (b, 0, 0), pipeline_mode=pl.Buffered(1)),
                  _resident((1, width), lambda b, i: (0, 0))],
        out_specs=pl.BlockSpec((1, seq_len, width), lambda b, i: (b, 0, 0)),
        out_shape=jax.ShapeDtypeStruct((bsz, seq_len, width), BF16),
        scratch_shapes=[pltpu.VMEM((nt, 4 * width), F32), pltpu.VMEM((seq_len, LANES), F32)],
        compiler_params=_params("parallel", "arbitrary"),
        name="hyena_long_conv",
    )(tb["f4"], tb["g4"], u, hhat, tb["twc"], tb["tws"], x0c, skip.astype(F32).reshape(1, width))


def _sgu_kernel(zu_ref, zv_ref, g_ref, b_ref, w_ref, sb_ref, o_ref):
    tokens, width = zu_ref.shape[1], zu_ref.shape[2]
    u = _gelu(zu_ref[0].astype(F32))
    v = _layer_norm(_gelu(zv_ref[0].astype(F32)), g_ref[...], b_ref[...]).astype(BF16)
    lane = lax.broadcasted_iota(jnp.int32, (SG_CHUNK, LANES), 1)
    first = lane < LANES // 2
    for c in range(tokens // SG_CHUNK):
        rows = slice(c * SG_CHUNK, (c + 1) * SG_CHUNK)
        for p in range(width // LANES):
            cols = slice(p * LANES, (p + 1) * LANES)
            r = jnp.dot(w_ref[p], v[rows, cols], preferred_element_type=F32)
            sv = jnp.where(first, r[:SG_CHUNK], r[SG_CHUNK:]) + sb_ref[:, cols]
            o_ref[0, rows, cols] = (u[rows, cols] * sv).astype(o_ref.dtype)


def _spatial_gating(p3, ln_g, ln_b, sg_w, sg_b, tokens=512):
    bsz, seq_len, _ = p3.shape
    groups = sg_w.shape[0]
    width = ln_g.shape[0]
    gw = width // groups
    assert 2 * gw == LANES and sg_w.shape[1] == SG_CHUNK
    wst = sg_w.astype(BF16).reshape(groups // 2, 2 * SG_CHUNK, SG_CHUNK)
    sb = jnp.repeat(sg_b.astype(F32).T, gw, axis=1)
    blk = (1, tokens, width)
    return pl.pallas_call(
        _sgu_kernel,
        grid=(bsz, seq_len // tokens),
        in_specs=[pl.BlockSpec(blk, lambda b, i: (b, i, 0)),
                  pl.BlockSpec(blk, lambda b, i: (b, i, 1)),
                  _resident((1, width), lambda b, i: (0, 0)),
                  _resident((1, width), lambda b, i: (0, 0)),
                  _resident((groups // 2, 2 * SG_CHUNK, SG_CHUNK), lambda b, i: (0, 0, 0)),
                  _resident((SG_CHUNK, width), lambda b, i: (0, 0))],
        out_specs=pl.BlockSpec(blk, lambda b, i: (b, i, 0)),
        out_shape=jax.ShapeDtypeStruct((bsz, seq_len, width), BF16),
        compiler_params=_params("parallel", "parallel"),
        name="spatial_gating",
    )(p3, p3, ln_g.astype(F32).reshape(1, width), ln_b.astype(F32).reshape(1, width), wst, sb)


CV_PAD = 16
CV_ROWS = 64
CV_UNROLL = 4


def _conf_kernel(za_ref, zg_ref, w_ref, wb_ref, g_ref, b_ref, o_ref, apad_ref):
    seq_len, width = za_ref.shape[1], za_ref.shape[2]
    fill_rows = 256
    pad_tiles = CV_PAD // SUBLANES
    out_tiles = CV_ROWS // SUBLANES
    apad_ref[0:pad_tiles] = jnp.zeros((pad_tiles, SUBLANES, width), F32)
    apad_ref[seq_len // SUBLANES + pad_tiles:seq_len // SUBLANES + 2 * pad_tiles] = jnp.zeros(
        (pad_tiles, SUBLANES, width), F32)

    def fill(i, carry):
        r0 = pl.multiple_of(i * fill_rows, fill_rows)
        a = za_ref[0, pl.ds(r0, fill_rows), :].astype(F32)
        g = zg_ref[0, pl.ds(r0, fill_rows), :].astype(F32)
        apad_ref[pl.ds(i * (fill_rows // SUBLANES) + pad_tiles, fill_rows // SUBLANES)] = (
            (a * jax.nn.sigmoid(g)).reshape(fill_rows // SUBLANES, SUBLANES, width))
        return carry

    lax.fori_loop(0, seq_len // fill_rows, fill, 0)
    sub = lax.broadcasted_iota(jnp.int32, (out_tiles, SUBLANES, width), 1)

    def conv(i, carry):
        r0 = pl.multiple_of(i * CV_ROWS, CV_ROWS)
        off0 = CV_PAD - CV_KERNEL // 2
        acc = None
        for s in range(SUBLANES):
            part = None
            for q in range((off0 + CV_KERNEL - 1) // SUBLANES + 1):
                j = SUBLANES * q + s - off0
                if 0 <= j < CV_KERNEL:
                    term = w_ref[j][None] * apad_ref[pl.ds(i * out_tiles + q, out_tiles + 1)]
                    part = term if part is None else part + term
            if s:
                rot = pltpu.roll(part, SUBLANES - s, 1)
                part = jnp.where(sub < SUBLANES - s, rot[:out_tiles], rot[1:])
            else:
                part = part[:out_tiles]
            acc = part if acc is None else acc + part
        y = _layer_norm(acc + wb_ref[...][None], g_ref[...][None], b_ref[...][None])
        y = y * jax.nn.sigmoid(y)
        o_ref[0, pl.ds(r0, CV_ROWS), :] = y.reshape(CV_ROWS, width).astype(o_ref.dtype)
        return carry

    lax.fori_loop(0, seq_len // CV_ROWS, conv, 0, unroll=CV_UNROLL)


def _conformer_conv(p3, lane_block, dw_w, dw_b, ln_g, ln_b):
    bsz, seq_len, _ = p3.shape
    width = dw_w.shape[1]
    blk = (1, seq_len, width)
    vec = lambda a: a.astype(F32).reshape(1, width)
    w8 = jnp.broadcast_to(dw_w.astype(F32)[:, None, :], (CV_KERNEL, SUBLANES, width))
    return pl.pallas_call(
        _conf_kernel,
        grid=(bsz,),
        in_specs=[pl.BlockSpec(blk, lambda b: (b, 0, lane_block)),
                  pl.BlockSpec(blk, lambda b: (b, 0, lane_block + 1)),
                  _resident((CV_KERNEL, SUBLANES, width), lambda b: (0, 0, 0)),
                  _resident((1, width), lambda b: (0, 0)),
                  _resident((1, width), lambda b: (0, 0)),
                  _resident((1, width), lambda b: (0, 0))],
        out_specs=pl.BlockSpec(blk, lambda b: (b, 0, 0)),
        out_shape=jax.ShapeDtypeStruct((bsz, seq_len, width), BF16),
        scratch_shapes=[pltpu.VMEM(((seq_len + 2 * CV_PAD) // SUBLANES, SUBLANES, width), F32)],
        compiler_params=_params("parallel"),
        name="conformer_conv",
    )(p3, p3, w8, vec(dw_b), vec(ln_g), vec(ln_b))


def _out_mlp_kernel(x_ref, a_ref, b_ref, woa_ref, wob_ref, g_ref, w1_ref, w2_ref, gf_ref, o_ref,
                    *, hidden_chunk, final):
    x1 = (x_ref[...]
          + jnp.dot(a_ref[...], woa_ref[...], preferred_element_type=F32)
          + jnp.dot(b_ref[...], wob_ref[...], preferred_element_type=F32))
    hn = _rms(x1, g_ref[...]).astype(BF16)
    mlp = None
    for c in range(w1_ref.shape[1] // hidden_chunk):
        cols = slice(c * hidden_chunk, (c + 1) * hidden_chunk)
        t = jnp.maximum(jnp.dot(hn, w1_ref[:, cols], preferred_element_type=F32), 0.0)
        y = jnp.dot((t * t).astype(BF16), w2_ref[cols, :], preferred_element_type=F32)
        mlp = y if mlp is None else mlp + y
    acc = x1 + mlp
    if final:
        acc = _rms(acc, gf_ref[...])
    o_ref[...] = acc


def _out_mlp(x2, a, b, w_out, g, w1, w2, gf, final, tm=512, hidden_chunk=1024):
    m, d = x2.shape
    half = a.shape[1]
    hidden = w1.shape[1]
    return pl.pallas_call(
        functools.partial(_out_mlp_kernel, hidden_chunk=hidden_chunk, final=final),
        grid=(m // tm,),
        in_specs=[pl.BlockSpec((tm, d), lambda i: (i, 0)),
                  pl.BlockSpec((tm, half), lambda i: (i, 0)),
                  pl.BlockSpec((tm, half), lambda i: (i, 0)),
                  _resident((half, d), lambda i: (0, 0)),
                  _resident((half, d), lambda i: (1, 0)),
                  _resident((1, d), lambda i: (0, 0)),
                  _resident((d, hidden), lambda i: (0, 0)),
                  _resident((hidden, d), lambda i: (0, 0)),
                  _resident((1, d), lambda i: (0, 0))],
        out_specs=pl.BlockSpec((tm, d), lambda i: (i, 0)),
        out_shape=jax.ShapeDtypeStruct((m, d), F32),
        compiler_params=_params("parallel"),
        name="out_mlp",
    )(x2, a, b, w_out, w_out, g.reshape(1, d), w1, w2, gf.reshape(1, d))


def kernel(x, norm_g, ev_w_in, ev_rpb, hy_conv_w, hy_conv_b, hy_w1, hy_b1, hy_w2, hy_b2, hy_w3, hy_b3, hy_freq, hy_w_out, hy_skip, ev_w_out, od_w_in, sg_ln_g, sg_ln_b, sg_w, sg_b, cv_dw_w, cv_dw_b, cv_ln_g, cv_ln_b, od_w_out, mlp_w1, mlp_w2, final_g):
    bsz, seq_len, d = x.shape
    depth = norm_g.shape[0]
    m = bsz * seq_len
    x2 = x.astype(F32).reshape(m, d)
    na_width = ev_rpb.shape[1] * NA_HEAD_DIM
    for i in range(depth):
        j = i // 2
        if i % 2 == 0:
            p3 = _norm_proj(x2, norm_g[i, 0], ev_w_in[j].astype(BF16)).reshape(bsz, seq_len, -1)
            mix_a = _neighbourhood_attention(p3, ev_rpb[j])
            u, x0c = _hyena_pre(p3, 3 * na_width, hy_conv_w[j], hy_conv_b[j])
            h, ss = _hyena_filters(seq_len, hy_w1[j], hy_b1[j], hy_w2[j], hy_b2[j], hy_w3[j], hy_b3[j],
                                   hy_freq[j], hy_w_out[j])
            mix_b = _hyena_long_conv(u, x0c, h, ss, hy_skip[j])
            w_out = ev_w_out[j]
        else:
            p3 = _norm_proj(x2, norm_g[i, 0], od_w_in[j].astype(BF16)).reshape(bsz, seq_len, -1)
            mix_a = _spatial_gating(p3, sg_ln_g[j], sg_ln_b[j], sg_w[j], sg_b[j])
            mix_b = _conformer_conv(p3, 2, cv_dw_w[j], cv_dw_b[j], cv_ln_g[j], cv_ln_b[j])
            w_out = od_w_out[j]
        x2 = _out_mlp(x2, mix_a.reshape(m, -1), mix_b.reshape(m, -1), w_out.astype(BF16), norm_g[i, 1],
                      mlp_w1[i].astype(BF16), mlp_w2[i].astype(BF16), final_g, final=(i == depth - 1))
    return x2.reshape(bsz, seq_len, d).astype(x.dtype)
```
